```python
import jax, jax.numpy as jnp
from jax import lax
import numpy as np

D_MODEL = 1024
BATCH = 8
SEQ = 2048
DEPTH = 2
DEC_BATCH = 128
DEC_SEQ = 1
PAST_LEN = 16384
PAGE_SIZE = 128

N_MIXERS = 2
N_A_LAYERS = (DEPTH + N_MIXERS - 1) // N_MIXERS
N_B_LAYERS = DEPTH // N_MIXERS
SG_CHUNK = 128
SG_WIDTH = 2 * D_MODEL
SG_GROUPS = 8
SG_GROUP_DIM = SG_WIDTH // SG_GROUPS
GLA_HEADS = 4
GLA_KEY_DIM = D_MODEL // 2
GLA_VALUE_DIM = D_MODEL
GLA_HEAD_K = GLA_KEY_DIM // GLA_HEADS
GLA_HEAD_V = GLA_VALUE_DIM // GLA_HEADS
GLA_GATE_RANK = 16
GLA_GATE_NORMALIZER = 16.0
GLA_CHUNK = 64
GLA_IN_DIM = 2 * GLA_KEY_DIM + 2 * GLA_VALUE_DIM + GLA_GATE_RANK
D_FF = ((8 * D_MODEL + 3 * 256 - 1) // (3 * 256)) * 256
EPS = 1e-6

kernel_name = "hybrid_sgu_gla_decoder_step"


def rms_norm(x, g):
    xf = x.astype(jnp.float32)
    y = xf * lax.rsqrt(jnp.mean(xf * xf, axis=-1, keepdims=True) + EPS)
    return (y * g.astype(jnp.float32)).astype(x.dtype)


def layer_norm(x, g, b):
    xf = x.astype(jnp.float32)
    mu = jnp.mean(xf, axis=-1, keepdims=True)
    var = jnp.mean(jnp.square(xf - mu), axis=-1, keepdims=True)
    y = (xf - mu) * lax.rsqrt(var + EPS)
    return (y * g.astype(jnp.float32) + b.astype(jnp.float32)).astype(x.dtype)


def swiglu(x, w13, w2):
    a, b = jnp.split(x @ w13, 2, axis=-1)
    return (jax.nn.silu(a) * b) @ w2


def spatial_gate_mixer(x, w_in, ln_g, ln_b, w_s, b_s, w_out):
    B, L, _ = x.shape
    z = jax.nn.gelu(x @ w_in, approximate=False)
    u, v = jnp.split(z, 2, axis=-1)
    v = layer_norm(v, ln_g, ln_b)
    cw = min(SG_CHUNK, L)
    n_chunks = -(-L // cw)
    lp = n_chunks * cw
    vp = jnp.pad(v, ((0, 0), (0, lp - L), (0, 0))).reshape(B, n_chunks, cw, SG_GROUPS, SG_GROUP_DIM)
    mask = jnp.tril(jnp.ones((cw, cw), dtype=bool))
    ws = jnp.where(mask, w_s[:, :cw, :cw], 0).astype(v.dtype)
    bias = b_s[:, :cw].T[:, :, None].astype(v.dtype)
    mixed = jnp.einsum('gts,bnsgc->bntgc', ws, vp) + bias
    mixed = mixed.reshape(B, lp, SG_WIDTH)[:, :L]
    y = (u * mixed) @ w_out
    n_last = (L - 1) % SG_CHUNK + 1
    return y, v[:, L - n_last:]


def gla_chunk_scan(q, k, v, g, s0):
    B, L = q.shape[:2]
    c = min(GLA_CHUNK, L)
    n = -(-L // c)
    lp = n * c

    def blocks(a):
        a = jnp.pad(a, ((0, 0), (0, lp - L), (0, 0), (0, 0)))
        return a.reshape(B, n, c, a.shape[2], a.shape[3]).transpose(1, 0, 3, 2, 4)

    qc, kc, vc, gc = blocks(q), blocks(k), blocks(v), blocks(g)
    mask = jnp.tril(jnp.ones((c, c), dtype=bool))[:, :, None]

    def step(S, inp):
        qi, ki, vi, gi = inp
        b = jnp.cumsum(gi, axis=2)
        o_inter = jnp.einsum('bhtk,bhkv->bhtv', qi * jnp.exp(b), S)
        diff = b[:, :, :, None, :] - b[:, :, None, :, :]
        decay = jnp.exp(jnp.where(mask, diff, -jnp.inf))
        attn = jnp.einsum('bhtk,bhsk,bhtsk->bhts', qi, ki, decay)
        o = o_inter + jnp.einsum('bhts,bhsv->bhtv', attn, vi)
        b_last = b[:, :, -1:, :]
        k_dec = ki * jnp.exp(b_last - b)
        S = jnp.exp(b_last[:, :, 0, :])[..., None] * S + jnp.einsum('bhsk,bhsv->bhkv', k_dec, vi)
        return S, o

    S, o = lax.scan(step, s0, (qc, kc, vc, gc))
    o = o.transpose(1, 0, 3, 2, 4).reshape(B, lp, GLA_HEADS, GLA_HEAD_V)[:, :L]
    return o, S


def gla_mixer(x, w_in, w_gate_up, b_gate, out_norm_g, w_out, s0):
    B, L, _ = x.shape
    proj = x @ w_in
    q, k, v, g_out, r = jnp.split(
        proj, [GLA_KEY_DIM, 2 * GLA_KEY_DIM, 2 * GLA_KEY_DIM + GLA_VALUE_DIM, 2 * GLA_KEY_DIM + 2 * GLA_VALUE_DIM], axis=-1)
    log_a = jax.nn.log_sigmoid((r @ w_gate_up + b_gate).astype(jnp.float32)) / GLA_GATE_NORMALIZER
    heads_k = lambda a: a.astype(jnp.float32).reshape(B, L, GLA_HEADS, GLA_HEAD_K)
    qh = heads_k(q) * (GLA_HEAD_K ** -0.5)
    kh = heads_k(k)
    gh = heads_k(log_a)
    vh = v.astype(jnp.float32).reshape(B, L, GLA_HEADS, GLA_HEAD_V)
    o, S = gla_chunk_scan(qh, kh, vh, gh, s0)
    o = rms_norm(o, out_norm_g).reshape(B, L, GLA_VALUE_DIM).astype(x.dtype)
    o = o * jax.nn.silu(g_out)
    return o @ w_out, S


def trunk(x, gla_init, norm_g, ffn_w13, ffn_w2, sg_w_in, sg_ln_g, sg_ln_b, sg_w_s, sg_b_s, sg_w_out,
          gla_w_in, gla_w_gate_up, gla_b_gate, gla_out_norm_g, gla_w_out):
    v_rows, gla_states = [], []
    for i in range(DEPTH):
        j = i // N_MIXERS
        h = rms_norm(x, norm_g[i, 0])
        if i % N_MIXERS == 0:
            m, vr = spatial_gate_mixer(h, sg_w_in[j], sg_ln_g[j], sg_ln_b[j], sg_w_s[j], sg_b_s[j], sg_w_out[j])
            v_rows.append(vr)
        else:
            m, S = gla_mixer(h, gla_w_in[j], gla_w_gate_up[j], gla_b_gate[j], gla_out_norm_g[j], gla_w_out[j], gla_init[j])
            gla_states.append(S)
        x = x + rms_norm(m, norm_g[i, 1])
        h = rms_norm(x, norm_g[i, 2])
        x = x + rms_norm(swiglu(h, ffn_w13[i], ffn_w2[i]), norm_g[i, 3])
    return x, jnp.stack(v_rows), jnp.stack(gla_states)


def setup_inputs(seed: int = 0) -> dict:
    key = jax.random.key(seed)
    ks = jax.random.split(key, 20)
    nrm = lambda k, shape, s: jax.random.normal(k, shape, jnp.float32) * s
    return {
        "x_prompt": nrm(ks[0], (BATCH, SEQ, D_MODEL), 1.0),
        "x_sample": nrm(ks[1], (DEC_BATCH, DEC_SEQ, D_MODEL), 1.0),
        "state_gla": nrm(ks[2], (N_B_LAYERS, DEC_BATCH, GLA_HEADS, GLA_HEAD_K, GLA_HEAD_V), 1.0),
        "norm_g": 1.0 + nrm(ks[3], (DEPTH, 4, D_MODEL), 0.1),
        "ffn_w13": nrm(ks[4], (DEPTH, D_MODEL, 2 * D_FF), D_MODEL ** -0.5),
        "ffn_w2": nrm(ks[5], (DEPTH, D_FF, D_MODEL), D_FF ** -0.5),
        "sg_w_in": nrm(ks[6], (N_A_LAYERS, D_MODEL, 2 * SG_WIDTH), D_MODEL ** -0.5),
        "sg_ln_g": 1.0 + nrm(ks[7], (N_A_LAYERS, SG_WIDTH), 0.1),
        "sg_ln_b": nrm(ks[8], (N_A_LAYERS, SG_WIDTH), 0.02),
        "sg_w_s": nrm(ks[9], (N_A_LAYERS, SG_GROUPS, SG_CHUNK, SG_CHUNK), SG_CHUNK ** -0.5),
        "sg_b_s": 1.0 + nrm(ks[10], (N_A_LAYERS, SG_GROUPS, SG_CHUNK), 0.1),
        "sg_w_out": nrm(ks[11], (N_A_LAYERS, SG_WIDTH, D_MODEL), SG_WIDTH ** -0.5),
        "gla_w_in": nrm(ks[12], (N_B_LAYERS, D_MODEL, GLA_IN_DIM), D_MODEL ** -0.5),
        "gla_w_gate_up": nrm(ks[13], (N_B_LAYERS, GLA_GATE_RANK, GLA_KEY_DIM), GLA_GATE_RANK ** -0.5),
        "gla_b_gate": nrm(ks[14], (N_B_LAYERS, GLA_KEY_DIM), 0.1),
        "gla_out_norm_g": 1.0 + nrm(ks[15], (N_B_LAYERS, GLA_HEAD_V), 0.1),
        "gla_w_out": nrm(ks[16], (N_B_LAYERS, GLA_VALUE_DIM, D_MODEL), GLA_VALUE_DIM ** -0.5),
    }


def reference(x_prompt, x_sample, state_gla, norm_g, ffn_w13, ffn_w2, sg_w_in, sg_ln_g, sg_ln_b, sg_w_s,
              sg_b_s, sg_w_out, gla_w_in, gla_w_gate_up, gla_b_gate, gla_out_norm_g, gla_w_out):
    zero_state = jnp.zeros((N_B_LAYERS, x_prompt.shape[0], GLA_HEADS, GLA_HEAD_K, GLA_HEAD_V), jnp.float32)
    y_prompt, sg_v_prompt, gla_state_prompt = trunk(
        x_prompt, zero_state, norm_g, ffn_w13, ffn_w2, sg_w_in, sg_ln_g, sg_ln_b, sg_w_s, sg_b_s, sg_w_out,
        gla_w_in, gla_w_gate_up, gla_b_gate, gla_out_norm_g, gla_w_out)
    y_sample, sg_v_sample, gla_state_sample = trunk(
        x_sample, state_gla.astype(jnp.float32), norm_g, ffn_w13, ffn_w2, sg_w_in, sg_ln_g, sg_ln_b, sg_w_s, sg_b_s,
        sg_w_out, gla_w_in, gla_w_gate_up, gla_b_gate, gla_out_norm_g, gla_w_out)
    return (y_prompt, y_sample, sg_v_prompt, sg_v_sample,
            gla_state_prompt.astype(x_prompt.dtype), gla_state_sample.astype(state_gla.dtype))
```

```python
import functools

import numpy as np
import jax
import jax.numpy as jnp
from jax import lax
from jax.experimental import pallas as pl
from jax.experimental.pallas import tpu as pltpu

F32 = jnp.float32
BF16 = jnp.bfloat16

D_MODEL = 1024
SG_CHUNK = 128
SG_WIDTH = 2 * D_MODEL
SG_GROUPS = 8
SG_GROUP_DIM = SG_WIDTH // SG_GROUPS
GLA_HEADS = 4
GLA_KEY_DIM = D_MODEL // 2
GLA_VALUE_DIM = D_MODEL
GLA_HEAD_K = GLA_KEY_DIM // GLA_HEADS
GLA_HEAD_V = GLA_VALUE_DIM // GLA_HEADS
GLA_GATE_RANK = 16
GLA_GATE_NORMALIZER = 16.0
D_FF = 2816
EPS = 1e-6

LANES = 128
GLA_CHUNK = 128
GLA_LEVELS = 7
FFN_CHUNK = 256
VMEM_LIMIT_BYTES = 56 * 1024 * 1024

SGU_TOKENS = 256
FFN_TOKENS = 256
GLA_TOKENS = 256
DEC_TOKENS = 8


def _const_spec(shape):
    nd = len(shape)
    return pl.BlockSpec(shape, lambda *_: (0,) * nd, pipeline_mode=pl.Buffered(1))


def _params(n_grid):
    return pltpu.CompilerParams(
        dimension_semantics=("arbitrary",) * n_grid, vmem_limit_bytes=VMEM_LIMIT_BYTES)


def _rms(x, g):
    return x * lax.rsqrt(jnp.mean(x * x, axis=-1, keepdims=True) + EPS) * g


def _gelu(z):
    return 0.5 * z * (1.0 + lax.erf(z * np.float32(np.sqrt(0.5))))


def _silu(z):
    return z * jax.nn.sigmoid(z)


def _dot(a, b):
    return jnp.dot(a, b, preferred_element_type=F32)


def _dot_nt(a, b):
    return lax.dot_general(a, b, (((1,), (1,)), ((), ())), preferred_element_type=F32)


def _dot_tn(a, b):
    return lax.dot_general(a, b, (((0,), (0,)), ((), ())), preferred_element_type=F32)


def _sgu_core(x, g0, g1, win_ref, lng, lnb, wout_ref, mix_fn):
    h = _rms(x, g0).astype(BF16)
    zv = _gelu(_dot(h, win_ref[:, SG_WIDTH:]))
    mu = jnp.mean(zv, axis=-1, keepdims=True)
    zc = zv - mu
    var = jnp.mean(zc * zc, axis=-1, keepdims=True)
    vn = zc * lax.rsqrt(var + EPS) * lng + lnb
    acc = jnp.zeros(x.shape, F32)
    for g in range(SG_GROUPS):
        cols = slice(g * SG_GROUP_DIM, (g + 1) * SG_GROUP_DIM)
        u = _gelu(_dot(h, win_ref[:, cols]))
        p = (u * mix_fn(g, vn[:, cols])).astype(BF16)
        acc = acc + _dot(p, wout_ref[cols, :])
    return x + _rms(acc, g1), vn


def _sgu_prompt_kernel(x_ref, g0_ref, g1_ref, win_ref, lng_ref, lnb_ref, ws_ref, bst_ref, wout_ref,
                       xo_ref, v_ref):
    m = x_ref.shape[0]
    row = lax.broadcasted_iota(jnp.int32, (SG_CHUNK, SG_CHUNK), 0)
    col = lax.broadcasted_iota(jnp.int32, (SG_CHUNK, SG_CHUNK), 1)
    causal = row >= col

    def mix(g, vg):
        wsg = jnp.where(causal, ws_ref[g], jnp.zeros((), BF16))
        bias = bst_ref[:, g:g + 1]
        parts = []
        for c in range(m // SG_CHUNK):
            vc = vg[c * SG_CHUNK:(c + 1) * SG_CHUNK].astype(BF16)
            parts.append(_dot(wsg, vc) + bias)
        return jnp.concatenate(parts, axis=0)

    xo, vn = _sgu_core(x_ref[...], g0_ref[...], g1_ref[...], win_ref, lng_ref[...], lnb_ref[...],
                       wout_ref, mix)
    xo_ref[...] = xo

    @pl.when(pl.program_id(1) == pl.num_programs(1) - 1)
    def _():
        v_ref[...] = vn[m - SG_CHUNK:]


def _sgu_sample_kernel(x_ref, g0_ref, g1_ref, win_ref, lng_ref, lnb_ref, wdiag_ref, bias_ref, wout_ref,
                       xo_ref, v_ref):
    def mix(g, vg):
        cols = slice(g * SG_GROUP_DIM, (g + 1) * SG_GROUP_DIM)
        return vg * wdiag_ref[:, cols] + bias_ref[:, cols]

    xo, vn = _sgu_core(x_ref[...], g0_ref[...], g1_ref[...], win_ref, lng_ref[...], lnb_ref[...],
                       wout_ref, mix)
    xo_ref[...] = xo
    v_ref[...] = vn


def _sgu_prompt(x, g0, g1, win, lng, lnb, ws, bst, wout):
    b, l, d = x.shape
    tm = SGU_TOKENS
    assert l % tm == 0 and tm % SG_CHUNK == 0
    return pl.pallas_call(
        _sgu_prompt_kernel,
        grid=(b, l // tm),
        in_specs=[
            pl.BlockSpec((None, tm, d), lambda i, j: (i, j, 0)),
            _const_spec(g0.shape), _const_spec(g1.shape), _const_spec(win.shape),
            _const_spec(lng.shape), _const_spec(lnb.shape), _const_spec(ws.shape),
            _const_spec(bst.shape), _const_spec(wout.shape),
        ],
        out_specs=[
            pl.BlockSpec((None, tm, d), lambda i, j: (i, j, 0)),
            pl.BlockSpec((None, SG_CHUNK, SG_WIDTH), lambda i, j: (i, 0, 0)),
        ],
        out_shape=[
            jax.ShapeDtypeStruct((b, l, d), F32),
            jax.ShapeDtypeStruct((b, SG_CHUNK, SG_WIDTH), F32),
        ],
        compiler_params=_params(2),
        name="sgu_prompt",
    )(x, g0, g1, win, lng, lnb, ws, bst, wout)


def _sgu_sample(x, g0, g1, win, lng, lnb, wdiag, bias, wout):
    n, d = x.shape
    args = (x, g0, g1, win, lng, lnb, wdiag, bias, wout)
    return pl.pallas_call(
        _sgu_sample_kernel,
        grid=(1,),
        in_specs=[_const_spec(a.shape) for a in args],
        out_specs=[pl.BlockSpec((n, d), lambda i: (0, 0)), pl.BlockSpec((n, SG_WIDTH), lambda i: (0, 0))],
        out_shape=[jax.ShapeDtypeStruct((n, d), F32), jax.ShapeDtypeStruct((n, SG_WIDTH), F32)],
        compiler_params=_params(1),
        name="sgu_sample",
    )(*args)


def _ffn_kernel(x_ref, g2_ref, g3_ref, w13_ref, w2_ref, xo_ref):
    x = x_ref[...]
    h = _rms(x, g2_ref[...]).astype(BF16)
    acc = jnp.zeros(x.shape, F32)
    for j in range(D_FF // FFN_CHUNK):
        a = _dot(h, w13_ref[:, j * FFN_CHUNK:(j + 1) * FFN_CHUNK])
        b = _dot(h, w13_ref[:, D_FF + j * FFN_CHUNK:D_FF + (j + 1) * FFN_CHUNK])
        p = (_silu(a) * b).astype(BF16)
        acc = acc + _dot(p, w2_ref[j * FFN_CHUNK:(j + 1) * FFN_CHUNK, :])
    xo_ref[...] = x + _rms(acc, g3_ref[...])


def _ffn(x, g2, g3, w13, w2):
    t, d = x.shape
    tm = min(FFN_TOKENS, t)
    assert t % tm == 0
    return pl.pallas_call(
        _ffn_kernel,
        grid=(t // tm,),
        in_specs=[
            pl.BlockSpec((tm, d), lambda i: (i, 0)),
            _const_spec(g2.shape), _const_spec(g3.shape), _const_spec(w13.shape), _const_spec(w2.shape),
        ],
        out_specs=pl.BlockSpec((tm, d), lambda i: (i, 0)),
        out_shape=jax.ShapeDtypeStruct((t, d), F32),
        compiler_params=_params(1),
        name="ffn",
    )(x, g2, g3, w13, w2)


def _gla_constants():
    c = GLA_CHUNK
    u = np.arange(c)[None, :]
    t = np.arange(c)[:, None]
    blocks = [(u <= t), (u > t)]
    for lev in range(1, GLA_LEVELS + 1):
        m = 1 << (lev - 1)
        pos = t % (2 * m)
        mid = t - pos + m - 1
        second = pos >= m
        blocks.append(np.where(second, (u > mid) & (u <= t), (u > t) & (u <= mid)))
    mats = np.concatenate(blocks, axis=0).astype(np.float32)
    x = t ^ u
    top = np.floor(np.log2(np.maximum(x, 1))).astype(np.int32) + 1
    level = np.where(u == t, 0, np.where(u < t, top, -1)).astype(np.int32)
    return mats, level


def _log_decay(r, wgu_ref, bg):
    gate = _dot(r.astype(BF16), wgu_ref[...]) + bg
    return jax.nn.log_sigmoid(gate) * np.float32(1.0 / GLA_GATE_NORMALIZER)


def _gla_output(x, o, go, ong, wo_ref, g1):
    parts = []
    for hd in range(GLA_HEADS):
        oh = o[:, hd * GLA_HEAD_V:(hd + 1) * GLA_HEAD_V]
        parts.append(_rms(oh, ong))
    on = jnp.concatenate(parts, axis=1) * _silu(go)
    y = _dot(on.astype(BF16), wo_ref[...])
    return x + _rms(y, g1)


def _gla_prompt_kernel(x_ref, s0_ref, g0_ref, g1_ref, wq_ref, wk_ref, wv_ref, wg_ref, wr_ref, wgu_ref,
                       bg_ref, ong_ref, wo_ref, mats_ref, level_ref,
                       xo_ref, so_ref,
                       st_scr, q_scr, k_scr, ga_scr, v_scr, o_scr):
    j = pl.program_id(1)
    m = x_ref.shape[0]
    c = GLA_CHUNK

    @pl.when(j == 0)
    def _():
        for hd in range(GLA_HEADS):
            st_scr[hd] = s0_ref[hd].T

    x = x_ref[...]
    h = _rms(x, g0_ref[...]).astype(BF16)
    q_scr[...] = _dot(h, wq_ref[...]) * np.float32(GLA_HEAD_K ** -0.5)
    k_scr[...] = _dot(h, wk_ref[...])
    v_scr[...] = _dot(h, wv_ref[...]).astype(BF16)
    go = _dot(h, wg_ref[...])
    ga_scr[...] = _log_decay(_dot(h, wr_ref[...]), wgu_ref, bg_ref[...])

    level = level_ref[...]

    def chunk(ci, carry):
        rows = pl.ds(pl.multiple_of(ci * c, c), c)
        g = ga_scr[rows, :]
        g_hi = g.astype(BF16)
        g_lo = (g - g_hi.astype(F32)).astype(BF16)

        def factor(blk):
            mat = mats_ref[blk * c:(blk + 1) * c, :]
            return jnp.exp(_dot(mat, g_hi) + _dot(mat, g_lo))

        qc = q_scr[rows, :]
        kc = k_scr[rows, :]
        vc = v_scr[rows, :]
        f_pre = factor(0)
        q_state = (qc * f_pre).astype(BF16)
        k_state = (kc * factor(1)).astype(BF16)
        decay = f_pre[c - 1:c, :]

        q16 = qc.astype(BF16)
        k16 = kc.astype(BF16)
        attn = []
        for hd in range(GLA_HEADS):
            ks = slice(hd * GLA_HEAD_K, (hd + 1) * GLA_HEAD_K)
            attn.append(jnp.where(level == 0, _dot_nt(q16[:, ks], k16[:, ks]), 0.0))
        for lev in range(1, GLA_LEVELS + 1):
            f = factor(1 + lev)
            ql = (qc * f).astype(BF16)
            kl = (kc * f).astype(BF16)
            for hd in range(GLA_HEADS):
                ks = slice(hd * GLA_HEAD_K, (hd + 1) * GLA_HEAD_K)
                attn[hd] = jnp.where(level == lev, _dot_nt(ql[:, ks], kl[:, ks]), attn[hd])

        for hd in range(GLA_HEADS):
            ks = slice(hd * GLA_HEAD_K, (hd + 1) * GLA_HEAD_K)
            vs = slice(hd * GLA_HEAD_V, (hd + 1) * GLA_HEAD_V)
            st = st_scr[hd]
            o_h = _dot_nt(q_state[:, ks], st.astype(BF16)) + _dot(attn[hd].astype(BF16), vc[:, vs])
            o_scr[rows, vs] = o_h
            st_scr[hd] = st * decay[:, ks] + _dot_tn(vc[:, vs], k_state[:, ks])
        return carry

    lax.fori_loop(0, m // c, chunk, 0)

    xo_ref[...] = _gla_output(x, o_scr[...], go, ong_ref[...], wo_ref, g1_ref[...])

    @pl.when(j == pl.num_programs(1) - 1)
    def _():
        for hd in range(GLA_HEADS):
            so_ref[hd] = st_scr[hd].T


def _gla_prompt(x, s0, g0, g1, wq, wk, wv, wg, wr, wgu, bg, ong, wo, mats, level):
    b, l, d = x.shape
    tm = GLA_TOKENS
    assert l % tm == 0 and tm % GLA_CHUNK == 0
    consts = (g0, g1, wq, wk, wv, wg, wr, wgu, bg, ong, wo, mats, level)
    state_spec = pl.BlockSpec((None, GLA_HEADS, GLA_HEAD_K, GLA_HEAD_V), lambda i, j: (i, 0, 0, 0))
    return pl.pallas_call(
        _gla_prompt_kernel,
        grid=(b, l // tm),
        in_specs=[pl.BlockSpec((None, tm, d), lambda i, j: (i, j, 0)), state_spec]
        + [_const_spec(a.shape) for a in consts],
        out_specs=[pl.BlockSpec((None, tm, d), lambda i, j: (i, j, 0)), state_spec],
        out_shape=[
            jax.ShapeDtypeStruct((b, l, d), F32),
            jax.ShapeDtypeStruct((b, GLA_HEADS, GLA_HEAD_K, GLA_HEAD_V), F32),
        ],
        scratch_shapes=[
            pltpu.VMEM((GLA_HEADS, GLA_HEAD_V, GLA_HEAD_K), F32),
            pltpu.VMEM((tm, GLA_KEY_DIM), F32),
            pltpu.VMEM((tm, GLA_KEY_DIM), F32),
            pltpu.VMEM((tm, GLA_KEY_DIM), F32),
            pltpu.VMEM((tm, GLA_VALUE_DIM), BF16),
            pltpu.VMEM((tm, GLA_VALUE_DIM), F32),
        ],
        compiler_params=_params(2),
        name="gla_prompt",
    )(x, s0, *consts)


def _gla_sample_kernel(x_ref, s_ref, g0_ref, g1_ref, wqt_ref, wkt_ref, wv_ref, wg_ref, wrt_ref, wgut_ref,
                       bgt_ref, ong_ref, wo_ref,
                       xo_ref, so_ref, o_scr):
    x = x_ref[...]
    h = _rms(x, g0_ref[...]).astype(BF16)
    qt = _dot_nt(wqt_ref[...], h) * np.float32(GLA_HEAD_K ** -0.5)
    kt = _dot_nt(wkt_ref[...], h)
    v = _dot(h, wv_ref[...])
    go = _dot(h, wg_ref[...])
    rt = _dot_nt(wrt_ref[...], h)
    gate_t = _dot(wgut_ref[...], rt.astype(BF16)) + bgt_ref[...]
    at = jnp.exp(jax.nn.log_sigmoid(gate_t) * np.float32(1.0 / GLA_GATE_NORMALIZER))
    for tok in range(x.shape[0]):
        for hd in range(GLA_HEADS):
            ks = slice(hd * GLA_HEAD_K, (hd + 1) * GLA_HEAD_K)
            vs = slice(hd * GLA_HEAD_V, (hd + 1) * GLA_HEAD_V)
            s_new = s_ref[tok, hd] * at[ks, tok:tok + 1] + kt[ks, tok:tok + 1] * v[tok:tok + 1, vs]
            so_ref[tok, hd] = s_new
            o_scr[tok:tok + 1, vs] = jnp.sum(qt[ks, tok:tok + 1] * s_new, axis=0, keepdims=True)
    xo_ref[...] = _gla_output(x, o_scr[...], go, ong_ref[...], wo_ref, g1_ref[...])


def _gla_sample(x, s, g0, g1, wqt, wkt, wv, wg, wrt, wgut, bgt, ong, wo):
    n, d = x.shape
    tb = DEC_TOKENS
    assert n % tb == 0
    consts = (g0, g1, wqt, wkt, wv, wg, wrt, wgut, bgt, ong, wo)
    state_spec = pl.BlockSpec((tb, GLA_HEADS, GLA_HEAD_K, GLA_HEAD_V), lambda i: (i, 0, 0, 0))
    return pl.pallas_call(
        _gla_sample_kernel,
        grid=(n // tb,),
        in_specs=[pl.BlockSpec((tb, d), lambda i: (i, 0)), state_spec]
        + [_const_spec(a.shape) for a in consts],
        out_specs=[pl.BlockSpec((tb, d), lambda i: (i, 0)), state_spec],
        out_shape=[
            jax.ShapeDtypeStruct((n, d), F32),
            jax.ShapeDtypeStruct((n, GLA_HEADS, GLA_HEAD_K, GLA_HEAD_V), F32),
        ],
        scratch_shapes=[pltpu.VMEM((tb, GLA_VALUE_DIM), F32)],
        compiler_params=_params(1),
        name="gla_sample",
    )(x, s, *consts)


def kernel(x_prompt, x_sample, state_gla, norm_g, ffn_w13, ffn_w2, sg_w_in, sg_ln_g, sg_ln_b, sg_w_s,
           sg_b_s, sg_w_out, gla_w_in, gla_w_gate_up, gla_b_gate, gla_out_norm_g, gla_w_out):
    b, l, d = x_prompt.shape
    n = x_sample.shape[0]
    row = lambda a: a.reshape(1, -1)
    ng = lambda i, k: row(norm_g[i, k])

    w13 = ffn_w13.astype(BF16)
    w2 = ffn_w2.astype(BF16)
    win = sg_w_in[0].astype(BF16)
    wout = sg_w_out[0].astype(BF16)
    lng, lnb = row(sg_ln_g[0]), row(sg_ln_b[0])
    ws = sg_w_s[0].astype(BF16)
    bst = sg_b_s[0].T
    wdiag = row(jnp.repeat(sg_w_s[0, :, 0, 0], SG_GROUP_DIM))
    bias0 = row(jnp.repeat(sg_b_s[0, :, 0], SG_GROUP_DIM))

    gw = gla_w_in[0].astype(BF16)
    kd, vd = GLA_KEY_DIM, GLA_VALUE_DIM
    wq, wk = gw[:, :kd], gw[:, kd:2 * kd]
    wv, wg = gw[:, 2 * kd:2 * kd + vd], gw[:, 2 * kd + vd:2 * kd + 2 * vd]
    wr = jnp.pad(gw[:, 2 * kd + 2 * vd:], ((0, 0), (0, LANES - GLA_GATE_RANK)))
    wgu = jnp.pad(gla_w_gate_up[0].astype(BF16), ((0, LANES - GLA_GATE_RANK), (0, 0)))
    bg = row(gla_b_gate[0])
    ong = row(gla_out_norm_g[0])
    wo = gla_w_out[0].astype(BF16)
    mats_np, level_np = _gla_constants()
    mats = jnp.asarray(mats_np, BF16)
    level = jnp.asarray(level_np)

    x, sg_v_prompt = _sgu_prompt(x_prompt, ng(0, 0), ng(0, 1), win, lng, lnb, ws, bst, wout)
    x = _ffn(x.reshape(b * l, d), ng(0, 2), ng(0, 3), w13[0], w2[0]).reshape(b, l, d)
    zero_state = jnp.zeros((b, GLA_HEADS, GLA_HEAD_K, GLA_HEAD_V), F32)
    x, gla_state_prompt = _gla_prompt(x, zero_state, ng(1, 0), ng(1, 1), wq, wk, wv, wg, wr, wgu, bg,
                                      ong, wo, mats, level)
    y_prompt = _ffn(x.reshape(b * l, d), ng(1, 2), ng(1, 3), w13[1], w2[1]).reshape(b, l, d)

    xs = x_sample.reshape(n, d)
    xs, sg_v_sample = _sgu_sample(xs, ng(0, 0), ng(0, 1), win, lng, lnb, wdiag, bias0, wout)
    xs = _ffn(xs, ng(0, 2), ng(0, 3), w13[0], w2[0])
    xs, gla_state_sample = _gla_sample(xs, state_gla[0].astype(F32), ng(1, 0), ng(1, 1), wq.T, wk.T, wv, wg,
                                       wr.T, wgu.T, bg.reshape(-1, 1), ong, wo)
    y_sample = _ffn(xs, ng(1, 2), ng(1, 3), w13[1], w2[1]).reshape(n, 1, d)

    return (y_prompt, y_sample, sg_v_prompt[None], sg_v_sample.reshape(1, n, 1, SG_WIDTH),
            gla_state_prompt[None].astype(x_prompt.dtype), gla_state_sample[None].astype(state_gla.dtype))
```

```python
import functools

import numpy as np
import jax
import jax.numpy as jnp
from jax import lax
from jax.experimental import pallas as pl
from jax.experimental.pallas import tpu as pltpu

F32 = jnp.float32
BF16 = jnp.bfloat16

D_MODEL = 1024
SG_CHUNK = 128
SG_WIDTH = 2 * D_MODEL
SG_GROUPS = 8
SG_GROUP_DIM = SG_WIDTH // SG_GROUPS
GLA_HEADS = 4
GLA_KEY_DIM = D_MODEL // 2
GLA_VALUE_DIM = D_MODEL
GLA_HEAD_K = GLA_KEY_DIM // GLA_HEADS
GLA_HEAD_V = GLA_VALUE_DIM // GLA_HEADS
GLA_GATE_RANK = 16
GLA_GATE_NORMALIZER = 16.0
D_FF = 2816
EPS = 1e-6

LANES = 128
GLA_CHUNK = 128
GLA_LEVELS = 7
FFN_CHUNK = 256
VMEM_LIMIT_BYTES = 56 * 1024 * 1024

SGU_TOKENS = 512
FFN_TOKENS = 512
GLA_TOKENS = 512
DEC_TOKENS = 8


def _const_spec(shape):
    nd = len(shape)
    return pl.BlockSpec(shape, lambda *_: (0,) * nd, pipeline_mode=pl.Buffered(1))


def _params(n_grid):
    return pltpu.CompilerParams(
        dimension_semantics=("arbitrary",) * n_grid, vmem_limit_bytes=VMEM_LIMIT_BYTES)


def _rms(x, g):
    return x * lax.rsqrt(jnp.mean(x * x, axis=-1, keepdims=True) + EPS) * g


def _gelu(z):
    return 0.5 * z * (1.0 + lax.erf(z * np.float32(np.sqrt(0.5))))


def _silu(z):
    return z * jax.nn.sigmoid(z)


def _dot(a, b):
    return jnp.dot(a, b, preferred_element_type=F32)


def _dot_nt(a, b):
    return lax.dot_general(a, b, (((1,), (1,)), ((), ())), preferred_element_type=F32)


def _dot_tn(a, b):
    return lax.dot_general(a, b, (((0,), (0,)), ((), ())), preferred_element_type=F32)


def _sgu_core(x, g0, g1, win_ref, lng, lnb, wout_ref, mix_fn):
    h = _rms(x, g0).astype(BF16)
    zv = _gelu(_dot(h, win_ref[:, SG_WIDTH:]))
    mu = jnp.mean(zv, axis=-1, keepdims=True)
    zc = zv - mu
    var = jnp.mean(zc * zc, axis=-1, keepdims=True)
    vn = zc * lax.rsqrt(var + EPS) * lng + lnb
    acc = jnp.zeros(x.shape, F32)
    for g in range(SG_GROUPS):
        cols = slice(g * SG_GROUP_DIM, (g + 1) * SG_GROUP_DIM)
        u = _gelu(_dot(h, win_ref[:, cols]))
        p = (u * mix_fn(g, vn[:, cols])).astype(BF16)
        acc = acc + _dot(p, wout_ref[cols, :])
    return x + _rms(acc, g1), vn


def _sgu_prompt_kernel(x_ref, g0_ref, g1_ref, win_ref, lng_ref, lnb_ref, ws_ref, bst_ref, wout_ref,
                       xo_ref, v_ref):
    m = x_ref.shape[0]
    row = lax.broadcasted_iota(jnp.int32, (SG_CHUNK, SG_CHUNK), 0)
    col = lax.broadcasted_iota(jnp.int32, (SG_CHUNK, SG_CHUNK), 1)
    causal = row >= col

    def mix(g, vg):
        wsg = jnp.where(causal, ws_ref[g], jnp.zeros((), BF16))
        bias = bst_ref[:, g:g + 1]
        parts = []
        for c in range(m // SG_CHUNK):
            vc = vg[c * SG_CHUNK:(c + 1) * SG_CHUNK].astype(BF16)
            parts.append(_dot(wsg, vc) + bias)
        return jnp.concatenate(parts, axis=0)

    xo, vn = _sgu_core(x_ref[...], g0_ref[...], g1_ref[...], win_ref, lng_ref[...], lnb_ref[...],
                       wout_ref, mix)
    xo_ref[...] = xo

    @pl.when(pl.program_id(1) == pl.num_programs(1) - 1)
    def _():
        v_ref[...] = vn[m - SG_CHUNK:]


def _sgu_sample_kernel(x_ref, g0_ref, g1_ref, win_ref, lng_ref, lnb_ref, wdiag_ref, bias_ref, wout_ref,
                       xo_ref, v_ref):
    def mix(g, vg):
        cols = slice(g * SG_GROUP_DIM, (g + 1) * SG_GROUP_DIM)
        return vg * wdiag_ref[:, cols] + bias_ref[:, cols]

    xo, vn = _sgu_core(x_ref[...], g0_ref[...], g1_ref[...], win_ref, lng_ref[...], lnb_ref[...],
                       wout_ref, mix)
    xo_ref[...] = xo
    v_ref[...] = vn


def _sgu_prompt(x, g0, g1, win, lng, lnb, ws, bst, wout):
    b, l, d = x.shape
    tm = SGU_TOKENS
    assert l % tm == 0 and tm % SG_CHUNK == 0
    return pl.pallas_call(
        _sgu_prompt_kernel,
        grid=(b, l // tm),
        in_specs=[
            pl.BlockSpec((None, tm, d), lambda i, j: (i, j, 0)),
            _const_spec(g0.shape), _const_spec(g1.shape), _const_spec(win.shape),
            _const_spec(lng.shape), _const_spec(lnb.shape), _const_spec(ws.shape),
            _const_spec(bst.shape), _const_spec(wout.shape),
        ],
        out_specs=[
            pl.BlockSpec((None, tm, d), lambda i, j: (i, j, 0)),
            pl.BlockSpec((None, SG_CHUNK, SG_WIDTH), lambda i, j: (i, 0, 0)),
        ],
        out_shape=[
            jax.ShapeDtypeStruct((b, l, d), F32),
            jax.ShapeDtypeStruct((b, SG_CHUNK, SG_WIDTH), F32),
        ],
        compiler_params=_params(2),
        name="sgu_prompt",
    )(x, g0, g1, win, lng, lnb, ws, bst, wout)


def _sgu_sample(x, g0, g1, win, lng, lnb, wdiag, bias, wout):
    n, d = x.shape
    args = (x, g0, g1, win, lng, lnb, wdiag, bias, wout)
    return pl.pallas_call(
        _sgu_sample_kernel,
        grid=(1,),
        in_specs=[_const_spec(a.shape) for a in args],
        out_specs=[pl.BlockSpec((n, d), lambda i: (0, 0)), pl.BlockSpec((n, SG_WIDTH), lambda i: (0, 0))],
        out_shape=[jax.ShapeDtypeStruct((n, d), F32), jax.ShapeDtypeStruct((n, SG_WIDTH), F32)],
        compiler_params=_params(1),
        name="sgu_sample",
    )(*args)


def _ffn_kernel(x_ref, g2_ref, g3_ref, w13_ref, w2_ref, xo_ref):
    x = x_ref[...]
    h = _rms(x, g2_ref[...]).astype(BF16)
    acc = jnp.zeros(x.shape, F32)
    for j in range(D_FF // FFN_CHUNK):
        a = _dot(h, w13_ref[:, j * FFN_CHUNK:(j + 1) * FFN_CHUNK])
        b = _dot(h, w13_ref[:, D_FF + j * FFN_CHUNK:D_FF + (j + 1) * FFN_CHUNK])
        p = (_silu(a) * b).astype(BF16)
        acc = acc + _dot(p, w2_ref[j * FFN_CHUNK:(j + 1) * FFN_CHUNK, :])
    xo_ref[...] = x + _rms(acc, g3_ref[...])


def _ffn(x, g2, g3, w13, w2):
    t, d = x.shape
    tm = min(FFN_TOKENS, t)
    assert t % tm == 0
    return pl.pallas_call(
        _ffn_kernel,
        grid=(t // tm,),
        in_specs=[
            pl.BlockSpec((tm, d), lambda i: (i, 0)),
            _const_spec(g2.shape), _const_spec(g3.shape), _const_spec(w13.shape), _const_spec(w2.shape),
        ],
        out_specs=pl.BlockSpec((tm, d), lambda i: (i, 0)),
        out_shape=jax.ShapeDtypeStruct((t, d), F32),
        compiler_params=_params(1),
        name="ffn",
    )(x, g2, g3, w13, w2)


def _gla_constants():
    c = GLA_CHUNK
    u = np.arange(c)[None, :]
    t = np.arange(c)[:, None]
    blocks = [(u <= t), (u > t)]
    for lev in range(1, GLA_LEVELS + 1):
        m = 1 << (lev - 1)
        pos = t % (2 * m)
        mid = t - pos + m - 1
        second = pos >= m
        blocks.append(np.where(second, (u > mid) & (u <= t), (u > t) & (u <= mid)))
    mats = np.concatenate(blocks, axis=0).astype(np.float32)
    x = t ^ u
    top = np.floor(np.log2(np.maximum(x, 1))).astype(np.int32) + 1
    level = np.where(u == t, 0, np.where(u < t, top, -1)).astype(np.int32)
    return mats, level


def _log_decay(r, wgu_ref, bg):
    gate = _dot(r.astype(BF16), wgu_ref[...]) + bg
    return jax.nn.log_sigmoid(gate) * np.float32(1.0 / GLA_GATE_NORMALIZER)


def _gla_output(x, o, go, ong, wo_ref, g1):
    parts = []
    for hd in range(GLA_HEADS):
        oh = o[:, hd * GLA_HEAD_V:(hd + 1) * GLA_HEAD_V]
        parts.append(_rms(oh, ong))
    on = jnp.concatenate(parts, axis=1) * _silu(go)
    y = _dot(on.astype(BF16), wo_ref[...])
    return x + _rms(y, g1)


def _gla_prompt_kernel(x_ref, s0_ref, g0_ref, g1_ref, wq_ref, wk_ref, wv_ref, wg_ref, wr_ref, wgu_ref,
                       bg_ref, ong_ref, wo_ref, mats_ref, level_ref,
                       xo_ref, so_ref,
                       st_scr, q_scr, k_scr, ga_scr, v_scr, o_scr):
    j = pl.program_id(1)
    m = x_ref.shape[0]
    c = GLA_CHUNK

    @pl.when(j == 0)
    def _():
        for hd in range(GLA_HEADS):
            st_scr[hd] = s0_ref[hd].T

    x = x_ref[...]
    h = _rms(x, g0_ref[...]).astype(BF16)
    q_scr[...] = _dot(h, wq_ref[...]) * np.float32(GLA_HEAD_K ** -0.5)
    k_scr[...] = _dot(h, wk_ref[...])
    v_scr[...] = _dot(h, wv_ref[...]).astype(BF16)
    go = _dot(h, wg_ref[...])
    ga_scr[...] = _log_decay(_dot(h, wr_ref[...]), wgu_ref, bg_ref[...])

    level = level_ref[...]

    def chunk(ci, carry):
        rows = pl.ds(pl.multiple_of(ci * c, c), c)
        g = ga_scr[rows, :]
        g_hi = g.astype(BF16)
        g_lo = (g - g_hi.astype(F32)).astype(BF16)

        def factor(blk):
            mat = mats_ref[blk * c:(blk + 1) * c, :]
            return jnp.exp(_dot(mat, g_hi) + _dot(mat, g_lo))

        qc = q_scr[rows, :]
        kc = k_scr[rows, :]
        vc = v_scr[rows, :]
        f_pre = factor(0)
        q_state = (qc * f_pre).astype(BF16)
        k_state = (kc * factor(1)).astype(BF16)
        decay = f_pre[c - 1:c, :]

        q16 = qc.astype(BF16)
        k16 = kc.astype(BF16)
        attn = []
        for hd in range(GLA_HEADS):
            ks = slice(hd * GLA_HEAD_K, (hd + 1) * GLA_HEAD_K)
            attn.append(jnp.where(level == 0, _dot_nt(q16[:, ks], k16[:, ks]), 0.0))
        for lev in range(1, GLA_LEVELS + 1):
            f = factor(1 + lev)
            ql = (qc * f).astype(BF16)
            kl = (kc * f).astype(BF16)
            for hd in range(GLA_HEADS):
                ks = slice(hd * GLA_HEAD_K, (hd + 1) * GLA_HEAD_K)
                attn[hd] = jnp.where(level == lev, _dot_nt(ql[:, ks], kl[:, ks]), attn[hd])

        for hd in range(GLA_HEADS):
            ks = slice(hd * GLA_HEAD_K, (hd + 1) * GLA_HEAD_K)
            vs = slice(hd * GLA_HEAD_V, (hd + 1) * GLA_HEAD_V)
            st = st_scr[hd]
            o_h = _dot_nt(q_state[:, ks], st.astype(BF16)) + _dot(attn[hd].astype(BF16), vc[:, vs])
            o_scr[rows, vs] = o_h
            st_scr[hd] = st * decay[:, ks] + _dot_tn(vc[:, vs], k_state[:, ks])
        return carry

    lax.fori_loop(0, m // c, chunk, 0, unroll=True)

    xo_ref[...] = _gla_output(x, o_scr[...], go, ong_ref[...], wo_ref, g1_ref[...])

    @pl.when(j == pl.num_programs(1) - 1)
    def _():
        for hd in range(GLA_HEADS):
            so_ref[hd] = st_scr[hd].T


def _gla_prompt(x, s0, g0, g1, wq, wk, wv, wg, wr, wgu, bg, ong, wo, mats, level):
    b, l, d = x.shape
    tm = GLA_TOKENS
    assert l % tm == 0 and tm % GLA_CHUNK == 0
    consts = (g0, g1, wq, wk, wv, wg, wr, wgu, bg, ong, wo, mats, level)
    state_spec = pl.BlockSpec((None, GLA_HEADS, GLA_HEAD_K, GLA_HEAD_V), lambda i, j: (i, 0, 0, 0))
    return pl.pallas_call(
        _gla_prompt_kernel,
        grid=(b, l // tm),
        in_specs=[pl.BlockSpec((None, tm, d), lambda i, j: (i, j, 0)), state_spec]
        + [_const_spec(a.shape) for a in consts],
        out_specs=[pl.BlockSpec((None, tm, d), lambda i, j: (i, j, 0)), state_spec],
        out_shape=[
            jax.ShapeDtypeStruct((b, l, d), F32),
            jax.ShapeDtypeStruct((b, GLA_HEADS, GLA_HEAD_K, GLA_HEAD_V), F32),
        ],
        scratch_shapes=[
            pltpu.VMEM((GLA_HEADS, GLA_HEAD_V, GLA_HEAD_K), F32),
            pltpu.VMEM((tm, GLA_KEY_DIM), F32),
            pltpu.VMEM((tm, GLA_KEY_DIM), F32),
            pltpu.VMEM((tm, GLA_KEY_DIM), F32),
            pltpu.VMEM((tm, GLA_VALUE_DIM), BF16),
            pltpu.VMEM((tm, GLA_VALUE_DIM), F32),
        ],
        compiler_params=_params(2),
        name="gla_prompt",
    )(x, s0, *consts)


def _gla_sample_kernel(x_ref, s_ref, g0_ref, g1_ref, wqt_ref, wkt_ref, wv_ref, wg_ref, wrt_ref, wgut_ref,
                       bgt_ref, ong_ref, wo_ref,
                       xo_ref, so_ref, o_scr):
    x = x_ref[...]
    h = _rms(x, g0_ref[...]).astype(BF16)
    qt = _dot_nt(wqt_ref[...], h) * np.float32(GLA_HEAD_K ** -0.5)
    kt = _dot_nt(wkt_ref[...], h)
    v = _dot(h, wv_ref[...])
    go = _dot(h, wg_ref[...])
    rt = _dot_nt(wrt_ref[...], h)
    gate_t = _dot(wgut_ref[...], rt.astype(BF16)) + bgt_ref[...]
    at = jnp.exp(jax.nn.log_sigmoid(gate_t) * np.float32(1.0 / GLA_GATE_NORMALIZER))
    for tok in range(x.shape[0]):
        for hd in range(GLA_HEADS):
            ks = slice(hd * GLA_HEAD_K, (hd + 1) * GLA_HEAD_K)
            vs = slice(hd * GLA_HEAD_V, (hd + 1) * GLA_HEAD_V)
            s_new = s_ref[tok, hd] * at[ks, tok:tok + 1] + kt[ks, tok:tok + 1] * v[tok:tok + 1, vs]
            so_ref[tok, hd] = s_new
            o_scr[tok:tok + 1, vs] = jnp.sum(qt[ks, tok:tok + 1] * s_new, axis=0, keepdims=True)
    xo_ref[...] = _gla_output(x, o_scr[...], go, ong_ref[...], wo_ref, g1_ref[...])


def _gla_sample(x, s, g0, g1, wqt, wkt, wv, wg, wrt, wgut, bgt, ong, wo):
    n, d = x.shape
    tb = DEC_TOKENS
    assert n % tb == 0
    consts = (g0, g1, wqt, wkt, wv, wg, wrt, wgut, bgt, ong, wo)
    state_spec = pl.BlockSpec((tb, GLA_HEADS, GLA_HEAD_K, GLA_HEAD_V), lambda i: (i, 0, 0, 0))
    return pl.pallas_call(
        _gla_sample_kernel,
        grid=(n // tb,),
        in_specs=[pl.BlockSpec((tb, d), lambda i: (i, 0)), state_spec]
        + [_const_spec(a.shape) for a in consts],
        out_specs=[pl.BlockSpec((tb, d), lambda i: (i, 0)), state_spec],
        out_shape=[
            jax.ShapeDtypeStruct((n, d), F32),
            jax.ShapeDtypeStruct((n, GLA_HEADS, GLA_HEAD_K, GLA_HEAD_V), F32),
        ],
        scratch_shapes=[pltpu.VMEM((tb, GLA_VALUE_DIM), F32)],
        compiler_params=_params(1),
        name="gla_sample",
    )(x, s, *consts)


def kernel(x_prompt, x_sample, state_gla, norm_g, ffn_w13, ffn_w2, sg_w_in, sg_ln_g, sg_ln_b, sg_w_s,
           sg_b_s, sg_w_out, gla_w_in, gla_w_gate_up, gla_b_gate, gla_out_norm_g, gla_w_out):
    b, l, d = x_prompt.shape
    n = x_sample.shape[0]
    row = lambda a: a.reshape(1, -1)
    ng = lambda i, k: row(norm_g[i, k])

    w13 = ffn_w13.astype(BF16)
    w2 = ffn_w2.astype(BF16)
    win = sg_w_in[0].astype(BF16)
    wout = sg_w_out[0].astype(BF16)
    lng, lnb = row(sg_ln_g[0]), row(sg_ln_b[0])
    ws = sg_w_s[0].astype(BF16)
    bst = sg_b_s[0].T
    wdiag = row(jnp.repeat(sg_w_s[0, :, 0, 0], SG_GROUP_DIM))
    bias0 = row(jnp.repeat(sg_b_s[0, :, 0], SG_GROUP_DIM))

    gw = gla_w_in[0].astype(BF16)
    kd, vd = GLA_KEY_DIM, GLA_VALUE_DIM
    wq, wk = gw[:, :kd], gw[:, kd:2 * kd]
    wv, wg = gw[:, 2 * kd:2 * kd + vd], gw[:, 2 * kd + vd:2 * kd + 2 * vd]
    wr = jnp.pad(gw[:, 2 * kd + 2 * vd:], ((0, 0), (0, LANES - GLA_GATE_RANK)))
    wgu = jnp.pad(gla_w_gate_up[0].astype(BF16), ((0, LANES - GLA_GATE_RANK), (0, 0)))
    bg = row(gla_b_gate[0])
    ong = row(gla_out_norm_g[0])
    wo = gla_w_out[0].astype(BF16)
    mats_np, level_np = _gla_constants()
    mats = jnp.asarray(mats_np, BF16)
    level = jnp.asarray(level_np)

    x, sg_v_prompt = _sgu_prompt(x_prompt, ng(0, 0), ng(0, 1), win, lng, lnb, ws, bst, wout)
    x = _ffn(x.reshape(b * l, d), ng(0, 2), ng(0, 3), w13[0], w2[0]).reshape(b, l, d)
    zero_state = jnp.zeros((b, GLA_HEADS, GLA_HEAD_K, GLA_HEAD_V), F32)
    x, gla_state_prompt = _gla_prompt(x, zero_state, ng(1, 0), ng(1, 1), wq, wk, wv, wg, wr, wgu, bg,
                                      ong, wo, mats, level)
    y_prompt = _ffn(x.reshape(b * l, d), ng(1, 2), ng(1, 3), w13[1], w2[1]).reshape(b, l, d)

    xs = x_sample.reshape(n, d)
    xs, sg_v_sample = _sgu_sample(xs, ng(0, 0), ng(0, 1), win, lng, lnb, wdiag, bias0, wout)
    xs = _ffn(xs, ng(0, 2), ng(0, 3), w13[0], w2[0])
    xs, gla_state_sample = _gla_sample(xs, state_gla[0].astype(F32), ng(1, 0), ng(1, 1), wq.T, wk.T, wv, wg,
                                       wr.T, wgu.T, bg.reshape(-1, 1), ong, wo)
    y_sample = _ffn(xs, ng(1, 2), ng(1, 3), w13[1], w2[1]).reshape(n, 1, d)

    return (y_prompt, y_sample, sg_v_prompt[None], sg_v_sample.reshape(1, n, 1, SG_WIDTH),
            gla_state_prompt[None].astype(x_prompt.dtype), gla_state_sample[None].astype(state_gla.dtype))
```

```python
import functools

import numpy as np
import jax
import jax.numpy as jnp
from jax import lax
from jax.experimental import pallas as pl
from jax.experimental.pallas import tpu as pltpu

F32 = jnp.float32
BF16 = jnp.bfloat16

D_MODEL = 1024
SG_CHUNK = 128
SG_WIDTH = 2 * D_MODEL
SG_GROUPS = 8
SG_GROUP_DIM = SG_WIDTH // SG_GROUPS
GLA_HEADS = 4
GLA_KEY_DIM = D_MODEL // 2
GLA_VALUE_DIM = D_MODEL
GLA_HEAD_K = GLA_KEY_DIM // GLA_HEADS
GLA_HEAD_V = GLA_VALUE_DIM // GLA_HEADS
GLA_GATE_RANK = 16
GLA_GATE_NORMALIZER = 16.0
D_FF = 2816
EPS = 1e-6

LANES = 128
SUBLANES = 8
GLA_CHUNK = 128
GLA_LEVELS = 7
FFN_CHUNK = 256
VMEM_LIMIT_BYTES = 56 * 1024 * 1024

SGU_TOKENS = 512
FFN_TOKENS = 512
GLA_TOKENS = 512
DEC_TOKENS = 8


def _const_spec(shape):
    nd = len(shape)
    return pl.BlockSpec(shape, lambda *_: (0,) * nd, pipeline_mode=pl.Buffered(1))


def _params(n_grid):
    return pltpu.CompilerParams(
        dimension_semantics=("arbitrary",) * n_grid, vmem_limit_bytes=VMEM_LIMIT_BYTES)


def _rms(x, g):
    return x * lax.rsqrt(jnp.mean(x * x, axis=-1, keepdims=True) + EPS) * g


def _gelu(z):
    return 0.5 * z * (1.0 + lax.erf(z * np.float32(np.sqrt(0.5))))


def _silu(z):
    return z * jax.nn.sigmoid(z)


def _dot(a, b):
    return jnp.dot(a, b, preferred_element_type=F32)


def _dot_nt(a, b):
    return lax.dot_general(a, b, (((1,), (1,)), ((), ())), preferred_element_type=F32)


def _dot_tn(a, b):
    return lax.dot_general(a, b, (((0,), (0,)), ((), ())), preferred_element_type=F32)


def _sgu_core(x, g0, g1, win_ref, lng_ref, lnb_ref, wout_ref, mix_fn, emit_v):
    h = _rms(x, g0).astype(BF16)
    inv_w = np.float32(1.0 / SG_WIDTH)
    zs, us = [], []
    for g in range(SG_GROUPS):
        us.append(_gelu(_dot(h, win_ref[:, g * SG_GROUP_DIM:(g + 1) * SG_GROUP_DIM])))
        z = _gelu(_dot(h, win_ref[:, SG_WIDTH + g * SG_GROUP_DIM:SG_WIDTH + (g + 1) * SG_GROUP_DIM]))
        if g == 0:
            shift = jnp.sum(z, axis=-1, keepdims=True) * np.float32(1.0 / SG_GROUP_DIM)
            s1 = jnp.zeros_like(shift)
            s2 = jnp.zeros_like(shift)
        z = z - shift
        s1 = s1 + jnp.sum(z, axis=-1, keepdims=True)
        s2 = s2 + jnp.sum(z * z, axis=-1, keepdims=True)
        zs.append(z)
    mu = s1 * inv_w
    rstd = lax.rsqrt(s2 * inv_w - mu * mu + EPS)
    acc = jnp.zeros(x.shape, F32)
    for g in range(SG_GROUPS):
        cols = slice(g * SG_GROUP_DIM, (g + 1) * SG_GROUP_DIM)
        vn = (zs[g] - mu) * rstd * lng_ref[:, cols] + lnb_ref[:, cols]
        emit_v(g, vn)
        p = (us[g] * mix_fn(g, vn)).astype(BF16)
        acc = acc + _dot(p, wout_ref[cols, :])
    return x + _rms(acc, g1)


def _sgu_prompt_kernel(x_ref, g0_ref, g1_ref, win_ref, lng_ref, lnb_ref, ws_ref, bst_ref, wout_ref,
                       xo_ref, v_ref):
    m = x_ref.shape[0]
    row = lax.broadcasted_iota(jnp.int32, (SG_CHUNK, SG_CHUNK), 0)
    col = lax.broadcasted_iota(jnp.int32, (SG_CHUNK, SG_CHUNK), 1)
    causal = row >= col

    def mix(g, vg):
        wsg = jnp.where(causal, ws_ref[g], jnp.zeros((), BF16))
        bias = bst_ref[:, g:g + 1]
        parts = []
        for c in range(m // SG_CHUNK):
            vc = vg[c * SG_CHUNK:(c + 1) * SG_CHUNK].astype(BF16)
            parts.append(_dot(wsg, vc) + bias)
        return jnp.concatenate(parts, axis=0)

    def emit_v(g, vn):
        v_ref[:, g * SG_GROUP_DIM:(g + 1) * SG_GROUP_DIM] = vn[m - SG_CHUNK:]

    xo_ref[...] = _sgu_core(x_ref[...], g0_ref[...], g1_ref[...], win_ref, lng_ref, lnb_ref,
                            wout_ref, mix, emit_v)


def _sgu_sample_kernel(x_ref, g0_ref, g1_ref, win_ref, lng_ref, lnb_ref, wdiag_ref, bias_ref, wout_ref,
                       xo_ref, v_ref):
    def mix(g, vg):
        cols = slice(g * SG_GROUP_DIM, (g + 1) * SG_GROUP_DIM)
        return vg * wdiag_ref[:, cols] + bias_ref[:, cols]

    def emit_v(g, vn):
        v_ref[:, g * SG_GROUP_DIM:(g + 1) * SG_GROUP_DIM] = vn

    xo_ref[...] = _sgu_core(x_ref[...], g0_ref[...], g1_ref[...], win_ref, lng_ref, lnb_ref,
                            wout_ref, mix, emit_v)


def _sgu_prompt(x, g0, g1, win, lng, lnb, ws, bst, wout):
    b, l, d = x.shape
    tm = SGU_TOKENS
    assert l % tm == 0 and tm % SG_CHUNK == 0
    return pl.pallas_call(
        _sgu_prompt_kernel,
        grid=(b, l // tm),
        in_specs=[
            pl.BlockSpec((None, tm, d), lambda i, j: (i, j, 0)),
            _const_spec(g0.shape), _const_spec(g1.shape), _const_spec(win.shape),
            _const_spec(lng.shape), _const_spec(lnb.shape), _const_spec(ws.shape),
            _const_spec(bst.shape), _const_spec(wout.shape),
        ],
        out_specs=[
            pl.BlockSpec((None, tm, d), lambda i, j: (i, j, 0)),
            pl.BlockSpec((None, SG_CHUNK, SG_WIDTH), lambda i, j: (i, 0, 0)),
        ],
        out_shape=[
            jax.ShapeDtypeStruct((b, l, d), F32),
            jax.ShapeDtypeStruct((b, SG_CHUNK, SG_WIDTH), F32),
        ],
        compiler_params=_params(2),
        name="sgu_prompt",
    )(x, g0, g1, win, lng, lnb, ws, bst, wout)


def _sgu_sample(x, g0, g1, win, lng, lnb, wdiag, bias, wout):
    n, d = x.shape
    args = (x, g0, g1, win, lng, lnb, wdiag, bias, wout)
    return pl.pallas_call(
        _sgu_sample_kernel,
        grid=(1,),
        in_specs=[_const_spec(a.shape) for a in args],
        out_specs=[pl.BlockSpec((n, d), lambda i: (0, 0)), pl.BlockSpec((n, SG_WIDTH), lambda i: (0, 0))],
        out_shape=[jax.ShapeDtypeStruct((n, d), F32), jax.ShapeDtypeStruct((n, SG_WIDTH), F32)],
        compiler_params=_params(1),
        name="sgu_sample",
    )(*args)


def _ffn_kernel(x_ref, g2_ref, g3_ref, w13_ref, w2_ref, xo_ref):
    x = x_ref[...]
    h = _rms(x, g2_ref[...]).astype(BF16)
    acc = jnp.zeros(x.shape, F32)
    for j in range(D_FF // FFN_CHUNK):
        a = _dot(h, w13_ref[:, j * FFN_CHUNK:(j + 1) * FFN_CHUNK])
        b = _dot(h, w13_ref[:, D_FF + j * FFN_CHUNK:D_FF + (j + 1) * FFN_CHUNK])
        p = (_silu(a) * b).astype(BF16)
        acc = acc + _dot(p, w2_ref[j * FFN_CHUNK:(j + 1) * FFN_CHUNK, :])
    xo_ref[...] = x + _rms(acc, g3_ref[...])


def _ffn(x, g2, g3, w13, w2):
    t, d = x.shape
    tm = min(FFN_TOKENS, t)
    assert t % tm == 0
    return pl.pallas_call(
        _ffn_kernel,
        grid=(t // tm,),
        in_specs=[
            pl.BlockSpec((tm, d), lambda i: (i, 0)),
            _const_spec(g2.shape), _const_spec(g3.shape), _const_spec(w13.shape), _const_spec(w2.shape),
        ],
        out_specs=pl.BlockSpec((tm, d), lambda i: (i, 0)),
        out_shape=jax.ShapeDtypeStruct((t, d), F32),
        compiler_params=_params(1),
        name="ffn",
    )(x, g2, g3, w13, w2)


def _gla_constants():
    c = GLA_CHUNK
    u = np.arange(c)[None, :]
    t = np.arange(c)[:, None]
    tri = (u <= t).astype(np.float32)
    top = np.floor(np.log2(np.maximum(t ^ u, 1))).astype(np.int32) + 1
    level = np.where(u == t, 0, np.where(u < t, top, -1)).astype(np.int32)
    return tri, level


def _mid_rows(b, lev):
    c, n = b.shape
    m = 1 << (lev - 1)
    if m >= SUBLANES:
        grp = 2 * m // SUBLANES
        b4 = b.reshape(c // (2 * m), grp, SUBLANES, n)
        mid = b4[:, grp // 2 - 1:grp // 2, SUBLANES - 1:SUBLANES, :]
        return jnp.broadcast_to(mid, b4.shape).reshape(c, n)
    b3 = b.reshape(c // SUBLANES, SUBLANES, n)
    sub = lax.broadcasted_iota(jnp.int32, b3.shape, 1)
    out = None
    for start in range(SUBLANES - 2 * m, -1, -2 * m):
        rowv = jnp.broadcast_to(b3[:, start + m - 1:start + m, :], b3.shape)
        out = rowv if out is None else jnp.where(sub < start + 2 * m, rowv, out)
    return out.reshape(c, n)


def _log_decay(r, wgu_ref, bg):
    gate = _dot(r.astype(BF16), wgu_ref[...]) + bg
    return jax.nn.log_sigmoid(gate) * np.float32(1.0 / GLA_GATE_NORMALIZER)


def _gla_output(x, o, go, ong, wo_ref, g1):
    parts = []
    for hd in range(GLA_HEADS):
        oh = o[:, hd * GLA_HEAD_V:(hd + 1) * GLA_HEAD_V]
        parts.append(_rms(oh, ong))
    on = jnp.concatenate(parts, axis=1) * _silu(go)
    y = _dot(on.astype(BF16), wo_ref[...])
    return x + _rms(y, g1)


def _gla_prompt_kernel(x_ref, s0_ref, g0_ref, g1_ref, wq_ref, wk_ref, wv_ref, wg_ref, wr_ref, wgu_ref,
                       bg_ref, ong_ref, wo_ref, tri_ref, level_ref,
                       xo_ref, so_ref,
                       st_scr, q_scr, k_scr, ga_scr, v_scr, o_scr):
    j = pl.program_id(1)
    m = x_ref.shape[0]
    c = GLA_CHUNK

    @pl.when(j == 0)
    def _():
        for hd in range(GLA_HEADS):
            st_scr[hd] = s0_ref[hd].T

    x = x_ref[...]
    h = _rms(x, g0_ref[...]).astype(BF16)
    q_scr[...] = _dot(h, wq_ref[...]) * np.float32(GLA_HEAD_K ** -0.5)
    k_scr[...] = _dot(h, wk_ref[...])
    v_scr[...] = _dot(h, wv_ref[...]).astype(BF16)
    go = _dot(h, wg_ref[...])
    ga_scr[...] = _log_decay(_dot(h, wr_ref[...]), wgu_ref, bg_ref[...])

    level = level_ref[...]

    def chunk(ci, carry):
        rows = pl.ds(pl.multiple_of(ci * c, c), c)
        g = ga_scr[rows, :]
        g_hi = g.astype(BF16)
        g_lo = (g - g_hi.astype(F32)).astype(BF16)
        b = _dot(tri_ref[...], g_hi) + _dot(tri_ref[...], g_lo)

        qc = q_scr[rows, :]
        kc = k_scr[rows, :]
        vc = v_scr[rows, :]
        f_pre = jnp.exp(b)
        q_state = (qc * f_pre).astype(BF16)
        k_state = (kc * jnp.exp(b[c - 1:c, :] - b)).astype(BF16)
        decay = f_pre[c - 1:c, :]

        q16 = qc.astype(BF16)
        k16 = kc.astype(BF16)
        attn = []
        for hd in range(GLA_HEADS):
            ks = slice(hd * GLA_HEAD_K, (hd + 1) * GLA_HEAD_K)
            attn.append(jnp.where(level == 0, _dot_nt(q16[:, ks], k16[:, ks]), 0.0))
        for lev in range(1, GLA_LEVELS + 1):
            f = jnp.exp(-jnp.abs(b - _mid_rows(b, lev)))
            ql = (qc * f).astype(BF16)
            kl = (kc * f).astype(BF16)
            for hd in range(GLA_HEADS):
                ks = slice(hd * GLA_HEAD_K, (hd + 1) * GLA_HEAD_K)
                attn[hd] = jnp.where(level == lev, _dot_nt(ql[:, ks], kl[:, ks]), attn[hd])

        for hd in range(GLA_HEADS):
            ks = slice(hd * GLA_HEAD_K, (hd + 1) * GLA_HEAD_K)
            vs = slice(hd * GLA_HEAD_V, (hd + 1) * GLA_HEAD_V)
            st = st_scr[hd]
            o_h = _dot_nt(q_state[:, ks], st.astype(BF16)) + _dot(attn[hd].astype(BF16), vc[:, vs])
            o_scr[rows, vs] = o_h
            st_scr[hd] = st * decay[:, ks] + _dot_tn(vc[:, vs], k_state[:, ks])
        return carry

    lax.fori_loop(0, m // c, chunk, 0, unroll=True)

    xo_ref[...] = _gla_output(x, o_scr[...], go, ong_ref[...], wo_ref, g1_ref[...])

    @pl.when(j == pl.num_programs(1) - 1)
    def _():
        for hd in range(GLA_HEADS):
            so_ref[hd] = st_scr[hd].T


def _gla_prompt(x, s0, g0, g1, wq, wk, wv, wg, wr, wgu, bg, ong, wo, tri, level):
    b, l, d = x.shape
    tm = GLA_TOKENS
    assert l % tm == 0 and tm % GLA_CHUNK == 0
    consts = (g0, g1, wq, wk, wv, wg, wr, wgu, bg, ong, wo, tri, level)
    state_spec = pl.BlockSpec((None, GLA_HEADS, GLA_HEAD_K, GLA_HEAD_V), lambda i, j: (i, 0, 0, 0))
    return pl.pallas_call(
        _gla_prompt_kernel,
        grid=(b, l // tm),
        in_specs=[pl.BlockSpec((None, tm, d), lambda i, j: (i, j, 0)), state_spec]
        + [_const_spec(a.shape) for a in consts],
        out_specs=[pl.BlockSpec((None, tm, d), lambda i, j: (i, j, 0)), state_spec],
        out_shape=[
            jax.ShapeDtypeStruct((b, l, d), F32),
            jax.ShapeDtypeStruct((b, GLA_HEADS, GLA_HEAD_K, GLA_HEAD_V), F32),
        ],
        scratch_shapes=[
            pltpu.VMEM((GLA_HEADS, GLA_HEAD_V, GLA_HEAD_K), F32),
            pltpu.VMEM((tm, GLA_KEY_DIM), F32),
            pltpu.VMEM((tm, GLA_KEY_DIM), F32),
            pltpu.VMEM((tm, GLA_KEY_DIM), F32),
            pltpu.VMEM((tm, GLA_VALUE_DIM), BF16),
            pltpu.VMEM((tm, GLA_VALUE_DIM), F32),
        ],
        compiler_params=_params(2),
        name="gla_prompt",
    )(x, s0, *consts)


def _gla_sample_kernel(x_ref, s_ref, g0_ref, g1_ref, wqt_ref, wkt_ref, wv_ref, wg_ref, wrt_ref, wgut_ref,
                       bgt_ref, ong_ref, wo_ref,
                       xo_ref, so_ref, o_scr):
    x = x_ref[...]
    h = _rms(x, g0_ref[...]).astype(BF16)
    qt = _dot_nt(wqt_ref[...], h) * np.float32(GLA_HEAD_K ** -0.5)
    kt = _dot_nt(wkt_ref[...], h)
    v = _dot(h, wv_ref[...])
    go = _dot(h, wg_ref[...])
    rt = _dot_nt(wrt_ref[...], h)
    gate_t = _dot(wgut_ref[...], rt.astype(BF16)) + bgt_ref[...]
    at = jnp.exp(jax.nn.log_sigmoid(gate_t) * np.float32(1.0 / GLA_GATE_NORMALIZER))
    for tok in range(x.shape[0]):
        for hd in range(GLA_HEADS):
            ks = slice(hd * GLA_HEAD_K, (hd + 1) * GLA_HEAD_K)
            vs = slice(hd * GLA_HEAD_V, (hd + 1) * GLA_HEAD_V)
            s_new = s_ref[tok, hd] * at[ks, tok:tok + 1] + kt[ks, tok:tok + 1] * v[tok:tok + 1, vs]
            so_ref[tok, hd] = s_new
            o_scr[tok:tok + 1, vs] = jnp.sum(qt[ks, tok:tok + 1] * s_new, axis=0, keepdims=True)
    xo_ref[...] = _gla_output(x, o_scr[...], go, ong_ref[...], wo_ref, g1_ref[...])


def _gla_sample(x, s, g0, g1, wqt, wkt, wv, wg, wrt, wgut, bgt, ong, wo):
    n, d = x.shape
    tb = DEC_TOKENS
    assert n % tb == 0
    consts = (g0, g1, wqt, wkt, wv, wg, wrt, wgut, bgt, ong, wo)
    state_spec = pl.BlockSpec((tb, GLA_HEADS, GLA_HEAD_K, GLA_HEAD_V), lambda i: (i, 0, 0, 0))
    return pl.pallas_call(
        _gla_sample_kernel,
        grid=(n // tb,),
        in_specs=[pl.BlockSpec((tb, d), lambda i: (i, 0)), state_spec]
        + [_const_spec(a.shape) for a in consts],
        out_specs=[pl.BlockSpec((tb, d), lambda i: (i, 0)), state_spec],
        out_shape=[
            jax.ShapeDtypeStruct((n, d), F32),
            jax.ShapeDtypeStruct((n, GLA_HEADS, GLA_HEAD_K, GLA_HEAD_V), F32),
        ],
        scratch_shapes=[pltpu.VMEM((tb, GLA_VALUE_DIM), F32)],
        compiler_params=_params(1),
        name="gla_sample",
    )(x, s, *consts)


def kernel(x_prompt, x_sample, state_gla, norm_g, ffn_w13, ffn_w2, sg_w_in, sg_ln_g, sg_ln_b, sg_w_s,
           sg_b_s, sg_w_out, gla_w_in, gla_w_gate_up, gla_b_gate, gla_out_norm_g, gla_w_out):
    b, l, d = x_prompt.shape
    n = x_sample.shape[0]
    row = lambda a: a.reshape(1, -1)
    ng = lambda i, k: row(norm_g[i, k])

    w13 = ffn_w13.astype(BF16)
    w2 = ffn_w2.astype(BF16)
    win = sg_w_in[0].astype(BF16)
    wout = sg_w_out[0].astype(BF16)
    lng, lnb = row(sg_ln_g[0]), row(sg_ln_b[0])
    ws = sg_w_s[0].astype(BF16)
    bst = sg_b_s[0].T
    wdiag = row(jnp.repeat(sg_w_s[0, :, 0, 0], SG_GROUP_DIM))
    bias0 = row(jnp.repeat(sg_b_s[0, :, 0], SG_GROUP_DIM))

    gw = gla_w_in[0].astype(BF16)
    kd, vd = GLA_KEY_DIM, GLA_VALUE_DIM
    wq, wk = gw[:, :kd], gw[:, kd:2 * kd]
    wv, wg = gw[:, 2 * kd:2 * kd + vd], gw[:, 2 * kd + vd:2 * kd + 2 * vd]
    wr = jnp.pad(gw[:, 2 * kd + 2 * vd:], ((0, 0), (0, LANES - GLA_GATE_RANK)))
    wgu = jnp.pad(gla_w_gate_up[0].astype(BF16), ((0, LANES - GLA_GATE_RANK), (0, 0)))
    bg = row(gla_b_gate[0])
    ong = row(gla_out_norm_g[0])
    wo = gla_w_out[0].astype(BF16)
    tri_np, level_np = _gla_constants()
    tri = jnp.asarray(tri_np, BF16)
    level = jnp.asarray(level_np)

    x, sg_v_prompt = _sgu_prompt(x_prompt, ng(0, 0), ng(0, 1), win, lng, lnb, ws, bst, wout)
    x = _ffn(x.reshape(b * l, d), ng(0, 2), ng(0, 3), w13[0], w2[0]).reshape(b, l, d)
    zero_state = jnp.zeros((b, GLA_HEADS, GLA_HEAD_K, GLA_HEAD_V), F32)
    x, gla_state_prompt = _gla_prompt(x, zero_state, ng(1, 0), ng(1, 1), wq, wk, wv, wg, wr, wgu, bg,
                                      ong, wo, tri, level)
    y_prompt = _ffn(x.reshape(b * l, d), ng(1, 2), ng(1, 3), w13[1], w2[1]).reshape(b, l, d)

    xs = x_sample.reshape(n, d)
    xs, sg_v_sample = _sgu_sample(xs, ng(0, 0), ng(0, 1), win, lng, lnb, wdiag, bias0, wout)
    xs = _ffn(xs, ng(0, 2), ng(0, 3), w13[0], w2[0])
    xs, gla_state_sample = _gla_sample(xs, state_gla[0].astype(F32), ng(1, 0), ng(1, 1), wq.T, wk.T, wv, wg,
                                       wr.T, wgu.T, bg.reshape(-1, 1), ong, wo)
    y_sample = _ffn(xs, ng(1, 2), ng(1, 3), w13[1], w2[1]).reshape(n, 1, d)

    return (y_prompt, y_sample, sg_v_prompt[None], sg_v_sample.reshape(1, n, 1, SG_WIDTH),
            gla_state_prompt[None].astype(x_prompt.dtype), gla_state_sample[None].astype(state_gla.dtype))
```

```python
import functools

import numpy as np
import jax
import jax.numpy as jnp
from jax import lax
from jax.experimental import pallas as pl
from jax.experimental.pallas import tpu as pltpu

F32 = jnp.float32
BF16 = jnp.bfloat16

D_MODEL = 1024
SG_CHUNK = 128
SG_WIDTH = 2 * D_MODEL
SG_GROUPS = 8
SG_GROUP_DIM = SG_WIDTH // SG_GROUPS
GLA_HEADS = 4
GLA_KEY_DIM = D_MODEL // 2
GLA_VALUE_DIM = D_MODEL
GLA_HEAD_K = GLA_KEY_DIM // GLA_HEADS
GLA_HEAD_V = GLA_VALUE_DIM // GLA_HEADS
GLA_GATE_RANK = 16
GLA_GATE_NORMALIZER = 16.0
D_FF = 2816
EPS = 1e-6

LANES = 128
SUBLANES = 8

GW_Q = 0
GW_K = GW_Q + GLA_KEY_DIM
GW_V = GW_K + GLA_KEY_DIM
GW_G = GW_V + GLA_VALUE_DIM
GW_R = GW_G + GLA_VALUE_DIM
GW_END = GW_R + LANES

GLA_CHUNK = 128
GLA_LEVELS = 7
FFN_CHUNK = 256
VMEM_LIMIT_BYTES = 56 * 1024 * 1024

SGU_TOKENS = 512
FFN_TOKENS = 1024
GLA_TOKENS = 512
DEC_TOKENS = 16


def _const_spec(shape):
    nd = len(shape)
    return pl.BlockSpec(shape, lambda *_: (0,) * nd, pipeline_mode=pl.Buffered(1))


def _params(n_grid):
    return pltpu.CompilerParams(
        dimension_semantics=("arbitrary",) * n_grid, vmem_limit_bytes=VMEM_LIMIT_BYTES)


def _rms(x, g):
    return x * lax.rsqrt(jnp.mean(x * x, axis=-1, keepdims=True) + EPS) * g


def _gelu(z):
    return 0.5 * z * (1.0 + lax.erf(z * np.float32(np.sqrt(0.5))))


def _silu(z):
    return z * jax.nn.sigmoid(z)


def _dot(a, b):
    return jnp.dot(a, b, preferred_element_type=F32)


def _dot_nt(a, b):
    return lax.dot_general(a, b, (((1,), (1,)), ((), ())), preferred_element_type=F32)


def _dot_tn(a, b):
    return lax.dot_general(a, b, (((0,), (0,)), ((), ())), preferred_element_type=F32)


def _sgu_core(x, g0, g1, win_ref, lng_ref, lnb_ref, wout_ref, mix_fn, emit_v):
    h = _rms(x, g0).astype(BF16)
    inv_w = np.float32(1.0 / SG_WIDTH)
    zs, us = [], []
    for g in range(SG_GROUPS):
        us.append(_gelu(_dot(h, win_ref[:, g * SG_GROUP_DIM:(g + 1) * SG_GROUP_DIM])))
        z = _gelu(_dot(h, win_ref[:, SG_WIDTH + g * SG_GROUP_DIM:SG_WIDTH + (g + 1) * SG_GROUP_DIM]))
        if g == 0:
            shift = jnp.sum(z, axis=-1, keepdims=True) * np.float32(1.0 / SG_GROUP_DIM)
            s1 = jnp.zeros_like(shift)
            s2 = jnp.zeros_like(shift)
        z = z - shift
        s1 = s1 + jnp.sum(z, axis=-1, keepdims=True)
        s2 = s2 + jnp.sum(z * z, axis=-1, keepdims=True)
        zs.append(z)
    mu = s1 * inv_w
    rstd = lax.rsqrt(s2 * inv_w - mu * mu + EPS)
    acc = jnp.zeros(x.shape, F32)
    for g in range(SG_GROUPS):
        cols = slice(g * SG_GROUP_DIM, (g + 1) * SG_GROUP_DIM)
        vn = (zs[g] - mu) * rstd * lng_ref[:, cols] + lnb_ref[:, cols]
        emit_v(g, vn)
        p = (us[g] * mix_fn(g, vn)).astype(BF16)
        acc = acc + _dot(p, wout_ref[cols, :])
    return x + _rms(acc, g1)


def _sgu_prompt_kernel(x_ref, g0_ref, g1_ref, win_ref, lng_ref, lnb_ref, ws_ref, bst_ref, wout_ref,
                       xo_ref, v_ref):
    m = x_ref.shape[0]
    row = lax.broadcasted_iota(jnp.int32, (SG_CHUNK, SG_CHUNK), 0)
    col = lax.broadcasted_iota(jnp.int32, (SG_CHUNK, SG_CHUNK), 1)
    causal = row >= col

    def mix(g, vg):
        wsg = jnp.where(causal, ws_ref[g], jnp.zeros((), BF16))
        bias = bst_ref[:, g:g + 1]
        parts = []
        for c in range(m // SG_CHUNK):
            vc = vg[c * SG_CHUNK:(c + 1) * SG_CHUNK].astype(BF16)
            parts.append(_dot(wsg, vc) + bias)
        return jnp.concatenate(parts, axis=0)

    def emit_v(g, vn):
        v_ref[:, g * SG_GROUP_DIM:(g + 1) * SG_GROUP_DIM] = vn[m - SG_CHUNK:]

    xo_ref[...] = _sgu_core(x_ref[...], g0_ref[...], g1_ref[...], win_ref, lng_ref, lnb_ref,
                            wout_ref, mix, emit_v)


def _sgu_sample_kernel(x_ref, g0_ref, g1_ref, win_ref, lng_ref, lnb_ref, wdiag_ref, bias_ref, wout_ref,
                       xo_ref, v_ref):
    def mix(g, vg):
        cols = slice(g * SG_GROUP_DIM, (g + 1) * SG_GROUP_DIM)
        return vg * wdiag_ref[:, cols] + bias_ref[:, cols]

    def emit_v(g, vn):
        v_ref[:, g * SG_GROUP_DIM:(g + 1) * SG_GROUP_DIM] = vn

    xo_ref[...] = _sgu_core(x_ref[...], g0_ref[...], g1_ref[...], win_ref, lng_ref, lnb_ref,
                            wout_ref, mix, emit_v)


def _sgu_prompt(x, g0, g1, win, lng, lnb, ws, bst, wout):
    b, l, d = x.shape
    tm = SGU_TOKENS
    assert l % tm == 0 and tm % SG_CHUNK == 0
    return pl.pallas_call(
        _sgu_prompt_kernel,
        grid=(b, l // tm),
        in_specs=[
            pl.BlockSpec((None, tm, d), lambda i, j: (i, j, 0)),
            _const_spec(g0.shape), _const_spec(g1.shape), _const_spec(win.shape),
            _const_spec(lng.shape), _const_spec(lnb.shape), _const_spec(ws.shape),
            _const_spec(bst.shape), _const_spec(wout.shape),
        ],
        out_specs=[
            pl.BlockSpec((None, tm, d), lambda i, j: (i, j, 0)),
            pl.BlockSpec((None, SG_CHUNK, SG_WIDTH), lambda i, j: (i, 0, 0)),
        ],
        out_shape=[
            jax.ShapeDtypeStruct((b, l, d), F32),
            jax.ShapeDtypeStruct((b, SG_CHUNK, SG_WIDTH), F32),
        ],
        compiler_params=_params(2),
        name="sgu_prompt",
    )(x, g0, g1, win, lng, lnb, ws, bst, wout)


def _sgu_sample(x, g0, g1, win, lng, lnb, wdiag, bias, wout):
    n, d = x.shape
    args = (x, g0, g1, win, lng, lnb, wdiag, bias, wout)
    return pl.pallas_call(
        _sgu_sample_kernel,
        grid=(1,),
        in_specs=[_const_spec(a.shape) for a in args],
        out_specs=[pl.BlockSpec((n, d), lambda i: (0, 0)), pl.BlockSpec((n, SG_WIDTH), lambda i: (0, 0))],
        out_shape=[jax.ShapeDtypeStruct((n, d), F32), jax.ShapeDtypeStruct((n, SG_WIDTH), F32)],
        compiler_params=_params(1),
        name="sgu_sample",
    )(*args)


def _ffn_kernel(x_ref, g2_ref, g3_ref, w13_ref, w2_ref, xo_ref):
    x = x_ref[...]
    h = _rms(x, g2_ref[...]).astype(BF16)
    acc = jnp.zeros(x.shape, F32)
    for j in range(D_FF // FFN_CHUNK):
        a = _dot(h, w13_ref[:, j * FFN_CHUNK:(j + 1) * FFN_CHUNK])
        b = _dot(h, w13_ref[:, D_FF + j * FFN_CHUNK:D_FF + (j + 1) * FFN_CHUNK])
        p = (_silu(a) * b).astype(BF16)
        acc = acc + _dot(p, w2_ref[j * FFN_CHUNK:(j + 1) * FFN_CHUNK, :])
    xo_ref[...] = x + _rms(acc, g3_ref[...])


def _ffn(x, g2, g3, w13, w2):
    t, d = x.shape
    tm = min(FFN_TOKENS, t)
    assert t % tm == 0
    return pl.pallas_call(
        _ffn_kernel,
        grid=(t // tm,),
        in_specs=[
            pl.BlockSpec((tm, d), lambda i: (i, 0)),
            _const_spec(g2.shape), _const_spec(g3.shape), _const_spec(w13.shape), _const_spec(w2.shape),
        ],
        out_specs=pl.BlockSpec((tm, d), lambda i: (i, 0)),
        out_shape=jax.ShapeDtypeStruct((t, d), F32),
        compiler_params=_params(1),
        name="ffn",
    )(x, g2, g3, w13, w2)


def _gla_constants():
    c = GLA_CHUNK
    u = np.arange(c)[None, :]
    t = np.arange(c)[:, None]
    tri = (u <= t).astype(np.float32)
    top = np.floor(np.log2(np.maximum(t ^ u, 1))).astype(np.int32) + 1
    level = np.where(u == t, 0, np.where(u < t, top, -1)).astype(np.int32)
    return tri, level


def _mid_rows(b, lev):
    c, n = b.shape
    m = 1 << (lev - 1)
    if m >= SUBLANES:
        grp = 2 * m // SUBLANES
        b4 = b.reshape(c // (2 * m), grp, SUBLANES, n)
        mid = b4[:, grp // 2 - 1:grp // 2, SUBLANES - 1:SUBLANES, :]
        return jnp.broadcast_to(mid, b4.shape).reshape(c, n)
    b3 = b.reshape(c // SUBLANES, SUBLANES, n)
    sub = lax.broadcasted_iota(jnp.int32, b3.shape, 1)
    out = None
    for start in range(SUBLANES - 2 * m, -1, -2 * m):
        rowv = jnp.broadcast_to(b3[:, start + m - 1:start + m, :], b3.shape)
        out = rowv if out is None else jnp.where(sub < start + 2 * m, rowv, out)
    return out.reshape(c, n)


def _log_decay(r, wgu_ref, bg):
    gate = _dot(r.astype(BF16), wgu_ref[...]) + bg
    return jax.nn.log_sigmoid(gate) * np.float32(1.0 / GLA_GATE_NORMALIZER)


def _gla_output(x, o, go, ong, wo_ref, g1):
    parts = []
    for hd in range(GLA_HEADS):
        oh = o[:, hd * GLA_HEAD_V:(hd + 1) * GLA_HEAD_V]
        parts.append(_rms(oh, ong))
    on = jnp.concatenate(parts, axis=1) * _silu(go)
    y = _dot(on.astype(BF16), wo_ref[...])
    return x + _rms(y, g1)


def _gla_prompt_kernel(x_ref, s0_ref, g0_ref, g1_ref, gw_ref, wgu_ref,
                       bg_ref, ong_ref, wo_ref, tri_ref, level_ref,
                       xo_ref, so_ref,
                       st_scr, q_scr, k_scr, ga_scr, v_scr, o_scr):
    j = pl.program_id(1)
    m = x_ref.shape[0]
    c = GLA_CHUNK

    @pl.when(j == 0)
    def _():
        for hd in range(GLA_HEADS):
            st_scr[hd] = s0_ref[hd].T

    x = x_ref[...]
    h = _rms(x, g0_ref[...]).astype(BF16)
    q_scr[...] = _dot(h, gw_ref[:, GW_Q:GW_K]) * np.float32(GLA_HEAD_K ** -0.5)
    k_scr[...] = _dot(h, gw_ref[:, GW_K:GW_V])
    v_scr[...] = _dot(h, gw_ref[:, GW_V:GW_G]).astype(BF16)
    go = _dot(h, gw_ref[:, GW_G:GW_R])
    ga_scr[...] = _log_decay(_dot(h, gw_ref[:, GW_R:GW_END]), wgu_ref, bg_ref[...])

    level = level_ref[...]

    def chunk(ci, carry):
        rows = pl.ds(pl.multiple_of(ci * c, c), c)
        g = ga_scr[rows, :]
        g_hi = g.astype(BF16)
        g_lo = (g - g_hi.astype(F32)).astype(BF16)
        b = _dot(tri_ref[...], g_hi) + _dot(tri_ref[...], g_lo)

        qc = q_scr[rows, :]
        kc = k_scr[rows, :]
        vc = v_scr[rows, :]
        f_pre = jnp.exp(b)
        q_state = (qc * f_pre).astype(BF16)
        k_state = (kc * jnp.exp(b[c - 1:c, :] - b)).astype(BF16)
        decay = f_pre[c - 1:c, :]

        q16 = qc.astype(BF16)
        k16 = kc.astype(BF16)
        attn = []
        for hd in range(GLA_HEADS):
            ks = slice(hd * GLA_HEAD_K, (hd + 1) * GLA_HEAD_K)
            attn.append(jnp.where(level == 0, _dot_nt(q16[:, ks], k16[:, ks]), 0.0))
        for lev in range(1, GLA_LEVELS + 1):
            f = jnp.exp(-jnp.abs(b - _mid_rows(b, lev))).astype(BF16)
            ql = q16 * f
            kl = k16 * f
            for hd in range(GLA_HEADS):
                ks = slice(hd * GLA_HEAD_K, (hd + 1) * GLA_HEAD_K)
                attn[hd] = jnp.where(level == lev, _dot_nt(ql[:, ks], kl[:, ks]), attn[hd])

        for hd in range(GLA_HEADS):
            ks = slice(hd * GLA_HEAD_K, (hd + 1) * GLA_HEAD_K)
            vs = slice(hd * GLA_HEAD_V, (hd + 1) * GLA_HEAD_V)
            st = st_scr[hd]
            o_h = _dot_nt(q_state[:, ks], st.astype(BF16)) + _dot(attn[hd].astype(BF16), vc[:, vs])
            o_scr[rows, vs] = o_h
            st_scr[hd] = st * decay[:, ks] + _dot_tn(vc[:, vs], k_state[:, ks])
        return carry

    lax.fori_loop(0, m // c, chunk, 0, unroll=True)

    xo_ref[...] = _gla_output(x, o_scr[...], go, ong_ref[...], wo_ref, g1_ref[...])

    @pl.when(j == pl.num_programs(1) - 1)
    def _():
        for hd in range(GLA_HEADS):
            so_ref[hd] = st_scr[hd].T


def _gla_prompt(x, s0, g0, g1, gw, wgu, bg, ong, wo, tri, level):
    b, l, d = x.shape
    tm = GLA_TOKENS
    assert l % tm == 0 and tm % GLA_CHUNK == 0
    consts = (g0, g1, gw, wgu, bg, ong, wo, tri, level)
    state_spec = pl.BlockSpec((None, GLA_HEADS, GLA_HEAD_K, GLA_HEAD_V), lambda i, j: (i, 0, 0, 0))
    return pl.pallas_call(
        _gla_prompt_kernel,
        grid=(b, l // tm),
        in_specs=[pl.BlockSpec((None, tm, d), lambda i, j: (i, j, 0)), state_spec]
        + [_const_spec(a.shape) for a in consts],
        out_specs=[pl.BlockSpec((None, tm, d), lambda i, j: (i, j, 0)), state_spec],
        out_shape=[
            jax.ShapeDtypeStruct((b, l, d), F32),
            jax.ShapeDtypeStruct((b, GLA_HEADS, GLA_HEAD_K, GLA_HEAD_V), F32),
        ],
        scratch_shapes=[
            pltpu.VMEM((GLA_HEADS, GLA_HEAD_V, GLA_HEAD_K), F32),
            pltpu.VMEM((tm, GLA_KEY_DIM), F32),
            pltpu.VMEM((tm, GLA_KEY_DIM), F32),
            pltpu.VMEM((tm, GLA_KEY_DIM), F32),
            pltpu.VMEM((tm, GLA_VALUE_DIM), BF16),
            pltpu.VMEM((tm, GLA_VALUE_DIM), F32),
        ],
        compiler_params=_params(2),
        name="gla_prompt",
    )(x, s0, *consts)


def _gla_sample_kernel(x_ref, s_ref, g0_ref, g1_ref, gw_ref, gwt_ref, wgut_ref,
                       bgt_ref, ong_ref, wo_ref,
                       xo_ref, so_ref, o_scr):
    x = x_ref[...]
    h = _rms(x, g0_ref[...]).astype(BF16)
    qt = _dot_nt(gwt_ref[GW_Q:GW_K, :], h) * np.float32(GLA_HEAD_K ** -0.5)
    kt = _dot_nt(gwt_ref[GW_K:GW_V, :], h)
    v = _dot(h, gw_ref[:, GW_V:GW_G])
    go = _dot(h, gw_ref[:, GW_G:GW_R])
    rt = _dot_nt(gwt_ref[GW_R:GW_END, :], h)
    gate_t = _dot(wgut_ref[...], rt.astype(BF16)) + bgt_ref[...]
    at = jnp.exp(jax.nn.log_sigmoid(gate_t) * np.float32(1.0 / GLA_GATE_NORMALIZER))
    for tok in range(x.shape[0]):
        for hd in range(GLA_HEADS):
            ks = slice(hd * GLA_HEAD_K, (hd + 1) * GLA_HEAD_K)
            vs = slice(hd * GLA_HEAD_V, (hd + 1) * GLA_HEAD_V)
            s_new = s_ref[tok, hd] * at[ks, tok:tok + 1] + kt[ks, tok:tok + 1] * v[tok:tok + 1, vs]
            so_ref[tok, hd] = s_new
            o_scr[tok:tok + 1, vs] = jnp.sum(qt[ks, tok:tok + 1] * s_new, axis=0, keepdims=True)
    xo_ref[...] = _gla_output(x, o_scr[...], go, ong_ref[...], wo_ref, g1_ref[...])


def _gla_sample(x, s, g0, g1, gw, gwt, wgut, bgt, ong, wo):
    n, d = x.shape
    tb = DEC_TOKENS
    assert n % tb == 0
    consts = (g0, g1, gw, gwt, wgut, bgt, ong, wo)
    state_spec = pl.BlockSpec((tb, GLA_HEADS, GLA_HEAD_K, GLA_HEAD_V), lambda i: (i, 0, 0, 0))
    return pl.pallas_call(
        _gla_sample_kernel,
        grid=(n // tb,),
        in_specs=[pl.BlockSpec((tb, d), lambda i: (i, 0)), state_spec]
        + [_const_spec(a.shape) for a in consts],
        out_specs=[pl.BlockSpec((tb, d), lambda i: (i, 0)), state_spec],
        out_shape=[
            jax.ShapeDtypeStruct((n, d), F32),
            jax.ShapeDtypeStruct((n, GLA_HEADS, GLA_HEAD_K, GLA_HEAD_V), F32),
        ],
        scratch_shapes=[pltpu.VMEM((tb, GLA_VALUE_DIM), F32)],
        compiler_params=_params(1),
        name="gla_sample",
    )(x, s, *consts)


def kernel(x_prompt, x_sample, state_gla, norm_g, ffn_w13, ffn_w2, sg_w_in, sg_ln_g, sg_ln_b, sg_w_s,
           sg_b_s, sg_w_out, gla_w_in, gla_w_gate_up, gla_b_gate, gla_out_norm_g, gla_w_out):
    b, l, d = x_prompt.shape
    n = x_sample.shape[0]
    row = lambda a: a.reshape(1, -1)
    ng = lambda i, k: row(norm_g[i, k])

    w13 = ffn_w13.astype(BF16)
    w2 = ffn_w2.astype(BF16)
    win = sg_w_in[0].astype(BF16)
    wout = sg_w_out[0].astype(BF16)
    lng, lnb = row(sg_ln_g[0]), row(sg_ln_b[0])
    ws = sg_w_s[0].astype(BF16)
    bst = sg_b_s[0].T
    wdiag = row(jnp.repeat(sg_w_s[0, :, 0, 0], SG_GROUP_DIM))
    bias0 = row(jnp.repeat(sg_b_s[0, :, 0], SG_GROUP_DIM))

    gw = jnp.pad(gla_w_in[0], ((0, 0), (0, LANES - GLA_GATE_RANK))).astype(BF16)
    wgu = jnp.pad(gla_w_gate_up[0].astype(BF16), ((0, LANES - GLA_GATE_RANK), (0, 0)))
    bg = row(gla_b_gate[0])
    ong = row(gla_out_norm_g[0])
    wo = gla_w_out[0].astype(BF16)
    tri_np, level_np = _gla_constants()
    tri = jnp.asarray(tri_np, BF16)
    level = jnp.asarray(level_np)

    x, sg_v_prompt = _sgu_prompt(x_prompt, ng(0, 0), ng(0, 1), win, lng, lnb, ws, bst, wout)
    x = _ffn(x.reshape(b * l, d), ng(0, 2), ng(0, 3), w13[0], w2[0]).reshape(b, l, d)
    zero_state = jnp.zeros((b, GLA_HEADS, GLA_HEAD_K, GLA_HEAD_V), F32)
    x, gla_state_prompt = _gla_prompt(x, zero_state, ng(1, 0), ng(1, 1), gw, wgu, bg, ong, wo, tri, level)
    y_prompt = _ffn(x.reshape(b * l, d), ng(1, 2), ng(1, 3), w13[1], w2[1]).reshape(b, l, d)

    xs = x_sample.reshape(n, d)
    xs, sg_v_sample = _sgu_sample(xs, ng(0, 0), ng(0, 1), win, lng, lnb, wdiag, bias0, wout)
    xs = _ffn(xs, ng(0, 2), ng(0, 3), w13[0], w2[0])
    xs, gla_state_sample = _gla_sample(xs, state_gla[0].astype(F32), ng(1, 0), ng(1, 1), gw, gw.T, wgu.T,
                                       bg.reshape(-1, 1), ong, wo)
    y_sample = _ffn(xs, ng(1, 2), ng(1, 3), w13[1], w2[1]).reshape(n, 1, d)

    return (y_prompt, y_sample, sg_v_prompt[None], sg_v_sample.reshape(1, n, 1, SG_WIDTH),
            gla_state_prompt[None].astype(x_prompt.dtype), gla_state_sample[None].astype(state_gla.dtype))
```

```python
import functools

import numpy as np
import jax
import jax.numpy as jnp
from jax import lax
from jax.experimental import pallas as pl
from jax.experimental.pallas import tpu as pltpu

F32 = jnp.float32
BF16 = jnp.bfloat16

D_MODEL = 1024
SG_CHUNK = 128
SG_WIDTH = 2 * D_MODEL
SG_GROUPS = 8
SG_GROUP_DIM = SG_WIDTH // SG_GROUPS
GLA_HEADS = 4
GLA_KEY_DIM = D_MODEL // 2
GLA_VALUE_DIM = D_MODEL
GLA_HEAD_K = GLA_KEY_DIM // GLA_HEADS
GLA_HEAD_V = GLA_VALUE_DIM // GLA_HEADS
GLA_GATE_RANK = 16
GLA_GATE_NORMALIZER = 16.0
D_FF = 2816
EPS = 1e-6

LANES = 128
SUBLANES = 8

GW_Q = 0
GW_K = GW_Q + GLA_KEY_DIM
GW_V = GW_K + GLA_KEY_DIM
GW_G = GW_V + GLA_VALUE_DIM
GW_R = GW_G + GLA_VALUE_DIM
GW_END = GW_R + LANES

GLA_CHUNK = 128
GLA_LEVELS = 7
FFN_CHUNK = 256
VMEM_LIMIT_BYTES = 56 * 1024 * 1024

SGU_TOKENS = 512
FFN_TOKENS = 1024
GLA_TOKENS = 512
DEC_TOKENS = 16


def _const_spec(shape):
    nd = len(shape)
    return pl.BlockSpec(shape, lambda *_: (0,) * nd, pipeline_mode=pl.Buffered(1))


def _params(n_grid):
    return pltpu.CompilerParams(
        dimension_semantics=("arbitrary",) * n_grid, vmem_limit_bytes=VMEM_LIMIT_BYTES)


def _rms(x, g):
    return x * lax.rsqrt(jnp.mean(x * x, axis=-1, keepdims=True) + EPS) * g


def _gelu(z):
    return 0.5 * z * (1.0 + lax.erf(z * np.float32(np.sqrt(0.5))))


def _silu(z):
    return z * jax.nn.sigmoid(z)


def _dot(a, b):
    return jnp.dot(a, b, preferred_element_type=F32)


def _dot_nt(a, b):
    return lax.dot_general(a, b, (((1,), (1,)), ((), ())), preferred_element_type=F32)


def _dot_tn(a, b):
    return lax.dot_general(a, b, (((0,), (0,)), ((), ())), preferred_element_type=F32)


def _sgu_core(x, g0, g1, win_ref, lng_ref, lnb_ref, wout_ref, mix_fn, emit_v):
    h = _rms(x, g0).astype(BF16)
    inv_w = np.float32(1.0 / SG_WIDTH)
    zs, us = [], []
    for g in range(SG_GROUPS):
        us.append(_gelu(_dot(h, win_ref[:, g * SG_GROUP_DIM:(g + 1) * SG_GROUP_DIM])))
        z = _gelu(_dot(h, win_ref[:, SG_WIDTH + g * SG_GROUP_DIM:SG_WIDTH + (g + 1) * SG_GROUP_DIM]))
        if g == 0:
            shift = jnp.sum(z, axis=-1, keepdims=True) * np.float32(1.0 / SG_GROUP_DIM)
            s1 = jnp.zeros_like(shift)
            s2 = jnp.zeros_like(shift)
        z = z - shift
        s1 = s1 + jnp.sum(z, axis=-1, keepdims=True)
        s2 = s2 + jnp.sum(z * z, axis=-1, keepdims=True)
        zs.append(z)
    mu = s1 * inv_w
    rstd = lax.rsqrt(s2 * inv_w - mu * mu + EPS)
    acc = jnp.zeros(x.shape, F32)
    for g in range(SG_GROUPS):
        cols = slice(g * SG_GROUP_DIM, (g + 1) * SG_GROUP_DIM)
        vn = (zs[g] - mu) * rstd * lng_ref[:, cols] + lnb_ref[:, cols]
        emit_v(g, vn)
        p = (us[g] * mix_fn(g, vn)).astype(BF16)
        acc = acc + _dot(p, wout_ref[cols, :])
    return x + _rms(acc, g1)


def _sgu_prompt_kernel(x_ref, g0_ref, g1_ref, win_ref, lng_ref, lnb_ref, ws_ref, bst_ref, wout_ref,
                       xo_ref, v_ref):
    m = x_ref.shape[0]
    row = lax.broadcasted_iota(jnp.int32, (SG_CHUNK, SG_CHUNK), 0)
    col = lax.broadcasted_iota(jnp.int32, (SG_CHUNK, SG_CHUNK), 1)
    causal = row >= col

    def mix(g, vg):
        wsg = jnp.where(causal, ws_ref[g], jnp.zeros((), BF16))
        bias = bst_ref[:, g:g + 1]
        parts = []
        for c in range(m // SG_CHUNK):
            vc = vg[c * SG_CHUNK:(c + 1) * SG_CHUNK].astype(BF16)
            parts.append(_dot(wsg, vc) + bias)
        return jnp.concatenate(parts, axis=0)

    def emit_v(g, vn):
        v_ref[:, g * SG_GROUP_DIM:(g + 1) * SG_GROUP_DIM] = vn[m - SG_CHUNK:]

    xo_ref[...] = _sgu_core(x_ref[...], g0_ref[...], g1_ref[...], win_ref, lng_ref, lnb_ref,
                            wout_ref, mix, emit_v)


def _sgu_sample_kernel(x_ref, g0_ref, g1_ref, win_ref, lng_ref, lnb_ref, wdiag_ref, bias_ref, wout_ref,
                       xo_ref, v_ref):
    def mix(g, vg):
        cols = slice(g * SG_GROUP_DIM, (g + 1) * SG_GROUP_DIM)
        return vg * wdiag_ref[:, cols] + bias_ref[:, cols]

    def emit_v(g, vn):
        v_ref[:, g * SG_GROUP_DIM:(g + 1) * SG_GROUP_DIM] = vn

    xo_ref[...] = _sgu_core(x_ref[...], g0_ref[...], g1_ref[...], win_ref, lng_ref, lnb_ref,
                            wout_ref, mix, emit_v)


def _sgu_prompt(x, g0, g1, win, lng, lnb, ws, bst, wout):
    b, l, d = x.shape
    tm = SGU_TOKENS
    assert l % tm == 0 and tm % SG_CHUNK == 0
    return pl.pallas_call(
        _sgu_prompt_kernel,
        grid=(b, l // tm),
        in_specs=[
            pl.BlockSpec((None, tm, d), lambda i, j: (i, j, 0)),
            _const_spec(g0.shape), _const_spec(g1.shape), _const_spec(win.shape),
            _const_spec(lng.shape), _const_spec(lnb.shape), _const_spec(ws.shape),
            _const_spec(bst.shape), _const_spec(wout.shape),
        ],
        out_specs=[
            pl.BlockSpec((None, tm, d), lambda i, j: (i, j, 0)),
            pl.BlockSpec((None, SG_CHUNK, SG_WIDTH), lambda i, j: (i, 0, 0)),
        ],
        out_shape=[
            jax.ShapeDtypeStruct((b, l, d), F32),
            jax.ShapeDtypeStruct((b, SG_CHUNK, SG_WIDTH), F32),
        ],
        compiler_params=_params(2),
        name="sgu_prompt",
    )(x, g0, g1, win, lng, lnb, ws, bst, wout)


def _sgu_sample(x, g0, g1, win, lng, lnb, wdiag, bias, wout):
    n, d = x.shape
    args = (x, g0, g1, win, lng, lnb, wdiag, bias, wout)
    return pl.pallas_call(
        _sgu_sample_kernel,
        grid=(1,),
        in_specs=[_const_spec(a.shape) for a in args],
        out_specs=[pl.BlockSpec((n, d), lambda i: (0, 0)), pl.BlockSpec((n, SG_WIDTH), lambda i: (0, 0))],
        out_shape=[jax.ShapeDtypeStruct((n, d), F32), jax.ShapeDtypeStruct((n, SG_WIDTH), F32)],
        compiler_params=_params(1),
        name="sgu_sample",
    )(*args)


def _ffn_kernel(x_ref, g2_ref, g3_ref, w13_ref, w2_ref, xo_ref):
    x = x_ref[...]
    h = _rms(x, g2_ref[...]).astype(BF16)
    acc = jnp.zeros(x.shape, F32)
    for j in range(D_FF // FFN_CHUNK):
        a = _dot(h, w13_ref[:, j * FFN_CHUNK:(j + 1) * FFN_CHUNK])
        b = _dot(h, w13_ref[:, D_FF + j * FFN_CHUNK:D_FF + (j + 1) * FFN_CHUNK])
        p = (_silu(a) * b).astype(BF16)
        acc = acc + _dot(p, w2_ref[j * FFN_CHUNK:(j + 1) * FFN_CHUNK, :])
    xo_ref[...] = x + _rms(acc, g3_ref[...])


def _ffn(x, g2, g3, w13, w2):
    t, d = x.shape
    tm = min(FFN_TOKENS, t)
    assert t % tm == 0
    return pl.pallas_call(
        _ffn_kernel,
        grid=(t // tm,),
        in_specs=[
            pl.BlockSpec((tm, d), lambda i: (i, 0)),
            _const_spec(g2.shape), _const_spec(g3.shape), _const_spec(w13.shape), _const_spec(w2.shape),
        ],
        out_specs=pl.BlockSpec((tm, d), lambda i: (i, 0)),
        out_shape=jax.ShapeDtypeStruct((t, d), F32),
        compiler_params=_params(1),
        name="ffn",
    )(x, g2, g3, w13, w2)


def _gla_constants():
    c = GLA_CHUNK
    u = np.arange(c)[None, :]
    t = np.arange(c)[:, None]
    tri = (u <= t).astype(np.float32)
    top = np.floor(np.log2(np.maximum(t ^ u, 1))).astype(np.int32) + 1
    level = np.where(u == t, 0, np.where(u < t, top, -1)).astype(np.int32)
    sign = np.stack([np.where(t[:, 0] % (2 << lev) >= (1 << lev), 1.0, -1.0) for lev in range(GLA_LEVELS)])
    sign = np.broadcast_to((sign * np.log2(np.e))[:, :, None], (GLA_LEVELS, c, GLA_KEY_DIM)).astype(np.float32)
    return tri, level, sign


def _mid_rows(b, lev):
    c, n = b.shape
    m = 1 << (lev - 1)
    if m >= SUBLANES:
        grp = 2 * m // SUBLANES
        b4 = b.reshape(c // (2 * m), grp, SUBLANES, n)
        mid = b4[:, grp // 2 - 1:grp // 2, SUBLANES - 1:SUBLANES, :]
        return jnp.broadcast_to(mid, b4.shape).reshape(c, n)
    b3 = b.reshape(c // SUBLANES, SUBLANES, n)
    sub = lax.broadcasted_iota(jnp.int32, b3.shape, 1)
    out = None
    for start in range(SUBLANES - 2 * m, -1, -2 * m):
        rowv = jnp.broadcast_to(b3[:, start + m - 1:start + m, :], b3.shape)
        out = rowv if out is None else jnp.where(sub < start + 2 * m, rowv, out)
    return out.reshape(c, n)


def _log_decay(r, wgu_ref, bg):
    gate = _dot(r.astype(BF16), wgu_ref[...]) + bg
    return jax.nn.log_sigmoid(gate) * np.float32(1.0 / GLA_GATE_NORMALIZER)


def _gla_output(x, o, go, ong, wo_ref, g1):
    parts = []
    for hd in range(GLA_HEADS):
        oh = o[:, hd * GLA_HEAD_V:(hd + 1) * GLA_HEAD_V]
        parts.append(_rms(oh, ong))
    on = jnp.concatenate(parts, axis=1) * _silu(go)
    y = _dot(on.astype(BF16), wo_ref[...])
    return x + _rms(y, g1)


def _gla_prompt_kernel(x_ref, s0_ref, g0_ref, g1_ref, gw_ref, wgu_ref,
                       bg_ref, ong_ref, wo_ref, tri_ref, level_ref, sign_ref,
                       xo_ref, so_ref,
                       st_scr, q_scr, k_scr, ga_scr, v_scr, o_scr):
    j = pl.program_id(1)
    m = x_ref.shape[0]
    c = GLA_CHUNK

    @pl.when(j == 0)
    def _():
        for hd in range(GLA_HEADS):
            st_scr[hd] = s0_ref[hd].T

    x = x_ref[...]
    h = _rms(x, g0_ref[...]).astype(BF16)
    q_scr[...] = _dot(h, gw_ref[:, GW_Q:GW_K]) * np.float32(GLA_HEAD_K ** -0.5)
    k_scr[...] = _dot(h, gw_ref[:, GW_K:GW_V])
    v_scr[...] = _dot(h, gw_ref[:, GW_V:GW_G]).astype(BF16)
    go = _dot(h, gw_ref[:, GW_G:GW_R])
    ga_scr[...] = _log_decay(_dot(h, gw_ref[:, GW_R:GW_END]), wgu_ref, bg_ref[...])

    level = level_ref[...]

    def chunk(ci, carry):
        rows = pl.ds(pl.multiple_of(ci * c, c), c)
        g = ga_scr[rows, :]
        g_hi = g.astype(BF16)
        g_lo = (g - g_hi.astype(F32)).astype(BF16)
        b = _dot(tri_ref[...], g_hi) + _dot(tri_ref[...], g_lo)

        qc = q_scr[rows, :]
        kc = k_scr[rows, :]
        vc = v_scr[rows, :]
        f_pre = jnp.exp(b)
        q_state = (qc * f_pre).astype(BF16)
        k_state = (kc * jnp.exp(b[c - 1:c, :] - b)).astype(BF16)
        decay = f_pre[c - 1:c, :]

        q16 = qc.astype(BF16)
        k16 = kc.astype(BF16)
        attn = []
        for hd in range(GLA_HEADS):
            ks = slice(hd * GLA_HEAD_K, (hd + 1) * GLA_HEAD_K)
            attn.append(jnp.where(level == 0, _dot_nt(q16[:, ks], k16[:, ks]), 0.0))
        for lev in range(1, GLA_LEVELS + 1):
            f = jnp.exp2((b - _mid_rows(b, lev)) * sign_ref[lev - 1]).astype(BF16)
            ql = q16 * f
            kl = k16 * f
            for hd in range(GLA_HEADS):
                ks = slice(hd * GLA_HEAD_K, (hd + 1) * GLA_HEAD_K)
                attn[hd] = jnp.where(level == lev, _dot_nt(ql[:, ks], kl[:, ks]), attn[hd])

        for hd in range(GLA_HEADS):
            ks = slice(hd * GLA_HEAD_K, (hd + 1) * GLA_HEAD_K)
            vs = slice(hd * GLA_HEAD_V, (hd + 1) * GLA_HEAD_V)
            st = st_scr[hd]
            o_h = _dot_nt(q_state[:, ks], st.astype(BF16)) + _dot(attn[hd].astype(BF16), vc[:, vs])
            o_scr[rows, vs] = o_h
            st_scr[hd] = st * decay[:, ks] + _dot_tn(vc[:, vs], k_state[:, ks])
        return carry

    lax.fori_loop(0, m // c, chunk, 0, unroll=True)

    xo_ref[...] = _gla_output(x, o_scr[...], go, ong_ref[...], wo_ref, g1_ref[...])

    @pl.when(j == pl.num_programs(1) - 1)
    def _():
        for hd in range(GLA_HEADS):
            so_ref[hd] = st_scr[hd].T


def _gla_prompt(x, s0, g0, g1, gw, wgu, bg, ong, wo, tri, level, sign):
    b, l, d = x.shape
    tm = GLA_TOKENS
    assert l % tm == 0 and tm % GLA_CHUNK == 0
    consts = (g0, g1, gw, wgu, bg, ong, wo, tri, level, sign)
    state_spec = pl.BlockSpec((None, GLA_HEADS, GLA_HEAD_K, GLA_HEAD_V), lambda i, j: (i, 0, 0, 0))
    return pl.pallas_call(
        _gla_prompt_kernel,
        grid=(b, l // tm),
        in_specs=[pl.BlockSpec((None, tm, d), lambda i, j: (i, j, 0)), state_spec]
        + [_const_spec(a.shape) for a in consts],
        out_specs=[pl.BlockSpec((None, tm, d), lambda i, j: (i, j, 0)), state_spec],
        out_shape=[
            jax.ShapeDtypeStruct((b, l, d), F32),
            jax.ShapeDtypeStruct((b, GLA_HEADS, GLA_HEAD_K, GLA_HEAD_V), F32),
        ],
        scratch_shapes=[
            pltpu.VMEM((GLA_HEADS, GLA_HEAD_V, GLA_HEAD_K), F32),
            pltpu.VMEM((tm, GLA_KEY_DIM), F32),
            pltpu.VMEM((tm, GLA_KEY_DIM), F32),
            pltpu.VMEM((tm, GLA_KEY_DIM), F32),
            pltpu.VMEM((tm, GLA_VALUE_DIM), BF16),
            pltpu.VMEM((tm, GLA_VALUE_DIM), F32),
        ],
        compiler_params=_params(2),
        name="gla_prompt",
    )(x, s0, *consts)


def _gla_sample_kernel(x_ref, s_ref, g0_ref, g1_ref, gw_ref, gwt_ref, wgut_ref,
                       bgt_ref, ong_ref, wo_ref,
                       xo_ref, so_ref, o_scr):
    x = x_ref[...]
    h = _rms(x, g0_ref[...]).astype(BF16)
    qt = _dot_nt(gwt_ref[GW_Q:GW_K, :], h) * np.float32(GLA_HEAD_K ** -0.5)
    kt = _dot_nt(gwt_ref[GW_K:GW_V, :], h)
    v = _dot(h, gw_ref[:, GW_V:GW_G])
    go = _dot(h, gw_ref[:, GW_G:GW_R])
    rt = _dot_nt(gwt_ref[GW_R:GW_END, :], h)
    gate_t = _dot(wgut_ref[...], rt.astype(BF16)) + bgt_ref[...]
    at = jnp.exp(jax.nn.log_sigmoid(gate_t) * np.float32(1.0 / GLA_GATE_NORMALIZER))
    for tok in range(x.shape[0]):
        for hd in range(GLA_HEADS):
            ks = slice(hd * GLA_HEAD_K, (hd + 1) * GLA_HEAD_K)
            vs = slice(hd * GLA_HEAD_V, (hd + 1) * GLA_HEAD_V)
            s_new = s_ref[tok, hd] * at[ks, tok:tok + 1] + kt[ks, tok:tok + 1] * v[tok:tok + 1, vs]
            so_ref[tok, hd] = s_new
            o_scr[tok:tok + 1, vs] = jnp.sum(qt[ks, tok:tok + 1] * s_new, axis=0, keepdims=True)
    xo_ref[...] = _gla_output(x, o_scr[...], go, ong_ref[...], wo_ref, g1_ref[...])


def _gla_sample(x, s, g0, g1, gw, gwt, wgut, bgt, ong, wo):
    n, d = x.shape
    tb = DEC_TOKENS
    assert n % tb == 0
    consts = (g0, g1, gw, gwt, wgut, bgt, ong, wo)
    state_spec = pl.BlockSpec((tb, GLA_HEADS, GLA_HEAD_K, GLA_HEAD_V), lambda i: (i, 0, 0, 0))
    return pl.pallas_call(
        _gla_sample_kernel,
        grid=(n // tb,),
        in_specs=[pl.BlockSpec((tb, d), lambda i: (i, 0)), state_spec]
        + [_const_spec(a.shape) for a in consts],
        out_specs=[pl.BlockSpec((tb, d), lambda i: (i, 0)), state_spec],
        out_shape=[
            jax.ShapeDtypeStruct((n, d), F32),
            jax.ShapeDtypeStruct((n, GLA_HEADS, GLA_HEAD_K, GLA_HEAD_V), F32),
        ],
        scratch_shapes=[pltpu.VMEM((tb, GLA_VALUE_DIM), F32)],
        compiler_params=_params(1),
        name="gla_sample",
    )(x, s, *consts)


def kernel(x_prompt, x_sample, state_gla, norm_g, ffn_w13, ffn_w2, sg_w_in, sg_ln_g, sg_ln_b, sg_w_s,
           sg_b_s, sg_w_out, gla_w_in, gla_w_gate_up, gla_b_gate, gla_out_norm_g, gla_w_out):
    b, l, d = x_prompt.shape
    n = x_sample.shape[0]
    row = lambda a: a.reshape(1, -1)
    ng = lambda i, k: row(norm_g[i, k])

    w13 = [ffn_w13[i].astype(BF16) for i in range(ffn_w13.shape[0])]
    w2 = [ffn_w2[i].astype(BF16) for i in range(ffn_w2.shape[0])]
    win = sg_w_in[0].astype(BF16)
    wout = sg_w_out[0].astype(BF16)
    lng, lnb = row(sg_ln_g[0]), row(sg_ln_b[0])
    ws = sg_w_s[0].astype(BF16)
    bst = sg_b_s[0].T
    wdiag = row(jnp.repeat(sg_w_s[0, :, 0, 0], SG_GROUP_DIM))
    bias0 = row(jnp.repeat(sg_b_s[0, :, 0], SG_GROUP_DIM))

    gw = jnp.pad(gla_w_in[0], ((0, 0), (0, LANES - GLA_GATE_RANK))).astype(BF16)
    wgu = jnp.pad(gla_w_gate_up[0].astype(BF16), ((0, LANES - GLA_GATE_RANK), (0, 0)))
    bg = row(gla_b_gate[0])
    ong = row(gla_out_norm_g[0])
    wo = gla_w_out[0].astype(BF16)
    tri_np, level_np, sign_np = _gla_constants()
    sign = jnp.asarray(sign_np)
    tri = jnp.asarray(tri_np, BF16)
    level = jnp.asarray(level_np)

    x, sg_v_prompt = _sgu_prompt(x_prompt, ng(0, 0), ng(0, 1), win, lng, lnb, ws, bst, wout)
    x = _ffn(x.reshape(b * l, d), ng(0, 2), ng(0, 3), w13[0], w2[0]).reshape(b, l, d)
    zero_state = jnp.zeros((b, GLA_HEADS, GLA_HEAD_K, GLA_HEAD_V), F32)
    x, gla_state_prompt = _gla_prompt(x, zero_state, ng(1, 0), ng(1, 1), gw, wgu, bg, ong, wo, tri, level,
                                      sign)
    y_prompt = _ffn(x.reshape(b * l, d), ng(1, 2), ng(1, 3), w13[1], w2[1]).reshape(b, l, d)

    xs = x_sample.reshape(n, d)
    xs, sg_v_sample = _sgu_sample(xs, ng(0, 0), ng(0, 1), win, lng, lnb, wdiag, bias0, wout)
    xs = _ffn(xs, ng(0, 2), ng(0, 3), w13[0], w2[0])
    xs, gla_state_sample = _gla_sample(xs, state_gla[0].astype(F32), ng(1, 0), ng(1, 1), gw, gw.T, wgu.T,
                                       bg.reshape(-1, 1), ong, wo)
    y_sample = _ffn(xs, ng(1, 2), ng(1, 3), w13[1], w2[1]).reshape(n, 1, d)

    return (y_prompt, y_sample, sg_v_prompt[None], sg_v_sample.reshape(1, n, 1, SG_WIDTH),
            gla_state_prompt[None].astype(x_prompt.dtype), gla_state_sample[None].astype(state_gla.dtype))
```

```python
import functools

import numpy as np
import jax
import jax.numpy as jnp
from jax import lax
from jax.experimental import pallas as pl
from jax.experimental.pallas import tpu as pltpu

F32 = jnp.float32
BF16 = jnp.bfloat16

D_MODEL = 1024
SG_CHUNK = 128
SG_WIDTH = 2 * D_MODEL
SG_GROUPS = 8
SG_GROUP_DIM = SG_WIDTH // SG_GROUPS
GLA_HEADS = 4
GLA_KEY_DIM = D_MODEL // 2
GLA_VALUE_DIM = D_MODEL
GLA_HEAD_K = GLA_KEY_DIM // GLA_HEADS
GLA_HEAD_V = GLA_VALUE_DIM // GLA_HEADS
GLA_GATE_RANK = 16
GLA_GATE_NORMALIZER = 16.0
D_FF = 2816
EPS = 1e-6

LANES = 128
SUBLANES = 8

GW_Q = 0
GW_K = GW_Q + GLA_KEY_DIM
GW_V = GW_K + GLA_KEY_DIM
GW_G = GW_V + GLA_VALUE_DIM
GW_R = GW_G + GLA_VALUE_DIM
GW_END = GW_R + LANES

GLA_CHUNK = 128
GLA_LEVELS = 7
FFN_CHUNK = 256
VMEM_LIMIT_BYTES = 56 * 1024 * 1024

SGU_TOKENS = 512
FFN_TOKENS = 1024
GLA_TOKENS = 512
DEC_TOKENS = 16


def _const_spec(shape):
    nd = len(shape)
    return pl.BlockSpec(shape, lambda *_: (0,) * nd, pipeline_mode=pl.Buffered(1))


def _params(n_grid):
    return pltpu.CompilerParams(
        dimension_semantics=("arbitrary",) * n_grid, vmem_limit_bytes=VMEM_LIMIT_BYTES)


def _rms(x, g):
    return x * lax.rsqrt(jnp.mean(x * x, axis=-1, keepdims=True) + EPS) * g


def _gelu(z):
    return 0.5 * z * (1.0 + lax.erf(z * np.float32(np.sqrt(0.5))))


def _silu(z):
    return z * jax.nn.sigmoid(z)


def _dot(a, b):
    return jnp.dot(a, b, preferred_element_type=F32)


def _dot_nt(a, b):
    return lax.dot_general(a, b, (((1,), (1,)), ((), ())), preferred_element_type=F32)


def _dot_tn(a, b):
    return lax.dot_general(a, b, (((0,), (0,)), ((), ())), preferred_element_type=F32)


def _sgu_core(x, g0, g1, win_ref, lng_ref, lnb_ref, wout_ref, mix_fn, emit_v):
    h = _rms(x, g0).astype(BF16)
    inv_w = np.float32(1.0 / SG_WIDTH)
    zs, us = [], []
    for g in range(SG_GROUPS):
        us.append(_gelu(_dot(h, win_ref[:, g * SG_GROUP_DIM:(g + 1) * SG_GROUP_DIM])))
        z = _gelu(_dot(h, win_ref[:, SG_WIDTH + g * SG_GROUP_DIM:SG_WIDTH + (g + 1) * SG_GROUP_DIM]))
        if g == 0:
            shift = jnp.sum(z, axis=-1, keepdims=True) * np.float32(1.0 / SG_GROUP_DIM)
            s1 = jnp.zeros_like(shift)
            s2 = jnp.zeros_like(shift)
        z = z - shift
        s1 = s1 + jnp.sum(z, axis=-1, keepdims=True)
        s2 = s2 + jnp.sum(z * z, axis=-1, keepdims=True)
        zs.append(z)
    mu = s1 * inv_w
    rstd = lax.rsqrt(s2 * inv_w - mu * mu + EPS)
    acc = jnp.zeros(x.shape, F32)
    for g in range(SG_GROUPS):
        cols = slice(g * SG_GROUP_DIM, (g + 1) * SG_GROUP_DIM)
        vn = (zs[g] - mu) * rstd * lng_ref[:, cols] + lnb_ref[:, cols]
        emit_v(g, vn)
        p = (us[g] * mix_fn(g, vn)).astype(BF16)
        acc = acc + _dot(p, wout_ref[cols, :])
    return x + _rms(acc, g1)


def _sgu_prompt_kernel(x_ref, g0_ref, g1_ref, win_ref, lng_ref, lnb_ref, ws_ref, bst_ref, wout_ref,
                       xo_ref, v_ref):
    m = x_ref.shape[0]
    row = lax.broadcasted_iota(jnp.int32, (SG_CHUNK, SG_CHUNK), 0)
    col = lax.broadcasted_iota(jnp.int32, (SG_CHUNK, SG_CHUNK), 1)
    causal = row >= col

    def mix(g, vg):
        wsg = jnp.where(causal, ws_ref[g], jnp.zeros((), BF16))
        bias = bst_ref[:, g:g + 1]
        parts = []
        for c in range(m // SG_CHUNK):
            vc = vg[c * SG_CHUNK:(c + 1) * SG_CHUNK].astype(BF16)
            parts.append(_dot(wsg, vc) + bias)
        return jnp.concatenate(parts, axis=0)

    def emit_v(g, vn):
        v_ref[:, g * SG_GROUP_DIM:(g + 1) * SG_GROUP_DIM] = vn[m - SG_CHUNK:]

    xo_ref[...] = _sgu_core(x_ref[...], g0_ref[...], g1_ref[...], win_ref, lng_ref, lnb_ref,
                            wout_ref, mix, emit_v)


def _sgu_sample_kernel(x_ref, g0_ref, g1_ref, win_ref, lng_ref, lnb_ref, wdiag_ref, bias_ref, wout_ref,
                       xo_ref, v_ref):
    def mix(g, vg):
        cols = slice(g * SG_GROUP_DIM, (g + 1) * SG_GROUP_DIM)
        return vg * wdiag_ref[:, cols] + bias_ref[:, cols]

    def emit_v(g, vn):
        v_ref[:, g * SG_GROUP_DIM:(g + 1) * SG_GROUP_DIM] = vn

    xo_ref[...] = _sgu_core(x_ref[...], g0_ref[...], g1_ref[...], win_ref, lng_ref, lnb_ref,
                            wout_ref, mix, emit_v)


def _sgu_prompt(x, g0, g1, win, lng, lnb, ws, bst, wout):
    b, l, d = x.shape
    tm = SGU_TOKENS
    assert l % tm == 0 and tm % SG_CHUNK == 0
    return pl.pallas_call(
        _sgu_prompt_kernel,
        grid=(b, l // tm),
        in_specs=[
            pl.BlockSpec((None, tm, d), lambda i, j: (i, j, 0)),
            _const_spec(g0.shape), _const_spec(g1.shape), _const_spec(win.shape),
            _const_spec(lng.shape), _const_spec(lnb.shape), _const_spec(ws.shape),
            _const_spec(bst.shape), _const_spec(wout.shape),
        ],
        out_specs=[
            pl.BlockSpec((None, tm, d), lambda i, j: (i, j, 0)),
            pl.BlockSpec((None, SG_CHUNK, SG_WIDTH), lambda i, j: (i, 0, 0)),
        ],
        out_shape=[
            jax.ShapeDtypeStruct((b, l, d), F32),
            jax.ShapeDtypeStruct((b, SG_CHUNK, SG_WIDTH), F32),
        ],
        compiler_params=_params(2),
        name="sgu_prompt",
    )(x, g0, g1, win, lng, lnb, ws, bst, wout)


def _sgu_sample(x, g0, g1, win, lng, lnb, wdiag, bias, wout):
    n, d = x.shape
    args = (x, g0, g1, win, lng, lnb, wdiag, bias, wout)
    return pl.pallas_call(
        _sgu_sample_kernel,
        grid=(1,),
        in_specs=[_const_spec(a.shape) for a in args],
        out_specs=[pl.BlockSpec((n, d), lambda i: (0, 0)), pl.BlockSpec((n, SG_WIDTH), lambda i: (0, 0))],
        out_shape=[jax.ShapeDtypeStruct((n, d), F32), jax.ShapeDtypeStruct((n, SG_WIDTH), F32)],
        compiler_params=_params(1),
        name="sgu_sample",
    )(*args)


def _ffn_kernel(x_ref, g2_ref, g3_ref, w13_ref, w2_ref, xo_ref):
    x = x_ref[...]
    h = _rms(x, g2_ref[...]).astype(BF16)
    acc = jnp.zeros(x.shape, F32)
    for j in range(D_FF // FFN_CHUNK):
        a = _dot(h, w13_ref[:, j * FFN_CHUNK:(j + 1) * FFN_CHUNK])
        b = _dot(h, w13_ref[:, D_FF + j * FFN_CHUNK:D_FF + (j + 1) * FFN_CHUNK])
        p = (_silu(a) * b).astype(BF16)
        acc = acc + _dot(p, w2_ref[j * FFN_CHUNK:(j + 1) * FFN_CHUNK, :])
    xo_ref[...] = x + _rms(acc, g3_ref[...])


def _layer_spec(w, layer):
    return pl.BlockSpec((None,) + w.shape[1:], lambda *_: (layer, 0, 0), pipeline_mode=pl.Buffered(1))


def _ffn(x, g2, g3, w13, w2, layer):
    t, d = x.shape
    tm = min(FFN_TOKENS, t)
    assert t % tm == 0
    return pl.pallas_call(
        _ffn_kernel,
        grid=(t // tm,),
        in_specs=[
            pl.BlockSpec((tm, d), lambda i: (i, 0)),
            _const_spec(g2.shape), _const_spec(g3.shape), _layer_spec(w13, layer), _layer_spec(w2, layer),
        ],
        out_specs=pl.BlockSpec((tm, d), lambda i: (i, 0)),
        out_shape=jax.ShapeDtypeStruct((t, d), F32),
        compiler_params=_params(1),
        name="ffn",
    )(x, g2, g3, w13, w2)


def _gla_constants():
    c = GLA_CHUNK
    u = np.arange(c)[None, :]
    t = np.arange(c)[:, None]
    tri = (u <= t).astype(np.float32)
    top = np.floor(np.log2(np.maximum(t ^ u, 1))).astype(np.int32) + 1
    level = np.where(u == t, 0, np.where(u < t, top, -1)).astype(np.int32)
    sign = np.stack([np.where(t[:, 0] % (2 << lev) >= (1 << lev), 1.0, -1.0) for lev in range(GLA_LEVELS)])
    sign = np.broadcast_to((sign * np.log2(np.e))[:, :, None], (GLA_LEVELS, c, GLA_KEY_DIM)).astype(np.float32)
    return tri, level, sign


def _mid_rows(b, lev):
    c, n = b.shape
    m = 1 << (lev - 1)
    if m >= SUBLANES:
        grp = 2 * m // SUBLANES
        b4 = b.reshape(c // (2 * m), grp, SUBLANES, n)
        mid = b4[:, grp // 2 - 1:grp // 2, SUBLANES - 1:SUBLANES, :]
        return jnp.broadcast_to(mid, b4.shape).reshape(c, n)
    b3 = b.reshape(c // SUBLANES, SUBLANES, n)
    sub = lax.broadcasted_iota(jnp.int32, b3.shape, 1)
    out = None
    for start in range(SUBLANES - 2 * m, -1, -2 * m):
        rowv = jnp.broadcast_to(b3[:, start + m - 1:start + m, :], b3.shape)
        out = rowv if out is None else jnp.where(sub < start + 2 * m, rowv, out)
    return out.reshape(c, n)


def _log_decay(r, wgu_ref, bg):
    gate = _dot(r.astype(BF16), wgu_ref[...]) + bg
    return jax.nn.log_sigmoid(gate) * np.float32(1.0 / GLA_GATE_NORMALIZER)


def _gla_output(x, o, go, ong, wo_ref, g1):
    parts = []
    for hd in range(GLA_HEADS):
        oh = o[:, hd * GLA_HEAD_V:(hd + 1) * GLA_HEAD_V]
        parts.append(_rms(oh, ong))
    on = jnp.concatenate(parts, axis=1) * _silu(go)
    y = _dot(on.astype(BF16), wo_ref[...])
    return x + _rms(y, g1)


def _gla_prompt_kernel(x_ref, s0_ref, g0_ref, g1_ref, gw_ref, wgu_ref,
                       bg_ref, ong_ref, wo_ref, tri_ref, level_ref, sign_ref,
                       xo_ref, so_ref,
                       st_scr, q_scr, k_scr, ga_scr, v_scr, o_scr):
    j = pl.program_id(1)
    m = x_ref.shape[0]
    c = GLA_CHUNK

    @pl.when(j == 0)
    def _():
        for hd in range(GLA_HEADS):
            st_scr[hd] = s0_ref[hd].T

    x = x_ref[...]
    h = _rms(x, g0_ref[...]).astype(BF16)
    q_scr[...] = _dot(h, gw_ref[:, GW_Q:GW_K]) * np.float32(GLA_HEAD_K ** -0.5)
    k_scr[...] = _dot(h, gw_ref[:, GW_K:GW_V])
    v_scr[...] = _dot(h, gw_ref[:, GW_V:GW_G]).astype(BF16)
    go = _dot(h, gw_ref[:, GW_G:GW_R])
    ga_scr[...] = _log_decay(_dot(h, gw_ref[:, GW_R:GW_END]), wgu_ref, bg_ref[...])

    level = level_ref[...]

    def chunk(ci, carry):
        rows = pl.ds(pl.multiple_of(ci * c, c), c)
        g = ga_scr[rows, :]
        g_hi = g.astype(BF16)
        g_lo = (g - g_hi.astype(F32)).astype(BF16)
        b = _dot(tri_ref[...], g_hi) + _dot(tri_ref[...], g_lo)

        qc = q_scr[rows, :]
        kc = k_scr[rows, :]
        vc = v_scr[rows, :]
        f_pre = jnp.exp(b)
        q_state = (qc * f_pre).astype(BF16)
        k_state = (kc * jnp.exp(b[c - 1:c, :] - b)).astype(BF16)
        decay = f_pre[c - 1:c, :]

        q16 = qc.astype(BF16)
        k16 = kc.astype(BF16)
        attn = []
        for hd in range(GLA_HEADS):
            ks = slice(hd * GLA_HEAD_K, (hd + 1) * GLA_HEAD_K)
            attn.append(jnp.where(level == 0, _dot_nt(q16[:, ks], k16[:, ks]), 0.0))
        for lev in range(1, GLA_LEVELS + 1):
            f = jnp.exp2((b - _mid_rows(b, lev)) * sign_ref[lev - 1]).astype(BF16)
            ql = q16 * f
            kl = k16 * f
            for hd in range(GLA_HEADS):
                ks = slice(hd * GLA_HEAD_K, (hd + 1) * GLA_HEAD_K)
                attn[hd] = jnp.where(level == lev, _dot_nt(ql[:, ks], kl[:, ks]), attn[hd])

        for hd in range(GLA_HEADS):
            ks = slice(hd * GLA_HEAD_K, (hd + 1) * GLA_HEAD_K)
            vs = slice(hd * GLA_HEAD_V, (hd + 1) * GLA_HEAD_V)
            st = st_scr[hd]
            o_h = _dot_nt(q_state[:, ks], st.astype(BF16)) + _dot(attn[hd].astype(BF16), vc[:, vs])
            o_scr[rows, vs] = o_h
            st_scr[hd] = st * decay[:, ks] + _dot_tn(vc[:, vs], k_state[:, ks])
        return carry

    lax.fori_loop(0, m // c, chunk, 0, unroll=True)

    xo_ref[...] = _gla_output(x, o_scr[...], go, ong_ref[...], wo_ref, g1_ref[...])

    @pl.when(j == pl.num_programs(1) - 1)
    def _():
        for hd in range(GLA_HEADS):
            so_ref[hd] = st_scr[hd].T


def _gla_prompt(x, s0, g0, g1, gw, wgu, bg, ong, wo, tri, level, sign):
    b, l, d = x.shape
    tm = GLA_TOKENS
    assert l % tm == 0 and tm % GLA_CHUNK == 0
    consts = (g0, g1, gw, wgu, bg, ong, wo, tri, level, sign)
    state_spec = pl.BlockSpec((None, GLA_HEADS, GLA_HEAD_K, GLA_HEAD_V), lambda i, j: (i, 0, 0, 0))
    return pl.pallas_call(
        _gla_prompt_kernel,
        grid=(b, l // tm),
        in_specs=[pl.BlockSpec((None, tm, d), lambda i, j: (i, j, 0)), state_spec]
        + [_const_spec(a.shape) for a in consts],
        out_specs=[pl.BlockSpec((None, tm, d), lambda i, j: (i, j, 0)), state_spec],
        out_shape=[
            jax.ShapeDtypeStruct((b, l, d), F32),
            jax.ShapeDtypeStruct((b, GLA_HEADS, GLA_HEAD_K, GLA_HEAD_V), F32),
        ],
        scratch_shapes=[
            pltpu.VMEM((GLA_HEADS, GLA_HEAD_V, GLA_HEAD_K), F32),
            pltpu.VMEM((tm, GLA_KEY_DIM), F32),
            pltpu.VMEM((tm, GLA_KEY_DIM), F32),
            pltpu.VMEM((tm, GLA_KEY_DIM), F32),
            pltpu.VMEM((tm, GLA_VALUE_DIM), BF16),
            pltpu.VMEM((tm, GLA_VALUE_DIM), F32),
        ],
        compiler_params=_params(2),
        name="gla_prompt",
    )(x, s0, *consts)


def _gla_sample_kernel(x_ref, s_ref, g0_ref, g1_ref, gw_ref, gwt_ref, wgut_ref,
                       bgt_ref, ong_ref, wo_ref,
                       xo_ref, so_ref, o_scr):
    x = x_ref[...]
    h = _rms(x, g0_ref[...]).astype(BF16)
    qt = _dot_nt(gwt_ref[GW_Q:GW_K, :], h) * np.float32(GLA_HEAD_K ** -0.5)
    kt = _dot_nt(gwt_ref[GW_K:GW_V, :], h)
    v = _dot(h, gw_ref[:, GW_V:GW_G])
    go = _dot(h, gw_ref[:, GW_G:GW_R])
    rt = _dot_nt(gwt_ref[GW_R:GW_END, :], h)
    gate_t = _dot(wgut_ref[...], rt.astype(BF16)) + bgt_ref[...]
    at = jnp.exp(jax.nn.log_sigmoid(gate_t) * np.float32(1.0 / GLA_GATE_NORMALIZER))
    for tok in range(x.shape[0]):
        for hd in range(GLA_HEADS):
            ks = slice(hd * GLA_HEAD_K, (hd + 1) * GLA_HEAD_K)
            vs = slice(hd * GLA_HEAD_V, (hd + 1) * GLA_HEAD_V)
            s_new = s_ref[tok, hd] * at[ks, tok:tok + 1] + kt[ks, tok:tok + 1] * v[tok:tok + 1, vs]
            so_ref[tok, hd] = s_new
            o_scr[tok:tok + 1, vs] = jnp.sum(qt[ks, tok:tok + 1] * s_new, axis=0, keepdims=True)
    xo_ref[...] = _gla_output(x, o_scr[...], go, ong_ref[...], wo_ref, g1_ref[...])


def _gla_sample(x, s, g0, g1, gw, gwt, wgut, bgt, ong, wo):
    n, d = x.shape
    tb = DEC_TOKENS
    assert n % tb == 0
    consts = (g0, g1, gw, gwt, wgut, bgt, ong, wo)
    state_spec = pl.BlockSpec((tb, GLA_HEADS, GLA_HEAD_K, GLA_HEAD_V), lambda i: (i, 0, 0, 0))
    return pl.pallas_call(
        _gla_sample_kernel,
        grid=(n // tb,),
        in_specs=[pl.BlockSpec((tb, d), lambda i: (i, 0)), state_spec]
        + [_const_spec(a.shape) for a in consts],
        out_specs=[pl.BlockSpec((tb, d), lambda i: (i, 0)), state_spec],
        out_shape=[
            jax.ShapeDtypeStruct((n, d), F32),
            jax.ShapeDtypeStruct((n, GLA_HEADS, GLA_HEAD_K, GLA_HEAD_V), F32),
        ],
        scratch_shapes=[pltpu.VMEM((tb, GLA_VALUE_DIM), F32)],
        compiler_params=_params(1),
        name="gla_sample",
    )(x, s, *consts)


def kernel(x_prompt, x_sample, state_gla, norm_g, ffn_w13, ffn_w2, sg_w_in, sg_ln_g, sg_ln_b, sg_w_s,
           sg_b_s, sg_w_out, gla_w_in, gla_w_gate_up, gla_b_gate, gla_out_norm_g, gla_w_out):
    b, l, d = x_prompt.shape
    n = x_sample.shape[0]
    row = lambda a: a.reshape(1, -1)
    ng = lambda i, k: row(norm_g[i, k])

    w13 = ffn_w13.astype(BF16)
    w2 = ffn_w2.astype(BF16)
    win = sg_w_in[0].astype(BF16)
    wout = sg_w_out[0].astype(BF16)
    lng, lnb = row(sg_ln_g[0]), row(sg_ln_b[0])
    ws = sg_w_s[0].astype(BF16)
    bst = sg_b_s[0].T
    wdiag = row(jnp.repeat(sg_w_s[0, :, 0, 0], SG_GROUP_DIM))
    bias0 = row(jnp.repeat(sg_b_s[0, :, 0], SG_GROUP_DIM))

    gw = jnp.pad(gla_w_in[0], ((0, 0), (0, LANES - GLA_GATE_RANK))).astype(BF16)
    wgu = jnp.pad(gla_w_gate_up[0].astype(BF16), ((0, LANES - GLA_GATE_RANK), (0, 0)))
    bg = row(gla_b_gate[0])
    ong = row(gla_out_norm_g[0])
    wo = gla_w_out[0].astype(BF16)
    tri_np, level_np, sign_np = _gla_constants()
    sign = jnp.asarray(sign_np)
    tri = jnp.asarray(tri_np, BF16)
    level = jnp.asarray(level_np)

    x, sg_v_prompt = _sgu_prompt(x_prompt, ng(0, 0), ng(0, 1), win, lng, lnb, ws, bst, wout)
    x = _ffn(x.reshape(b * l, d), ng(0, 2), ng(0, 3), w13, w2, 0).reshape(b, l, d)
    zero_state = jnp.zeros((b, GLA_HEADS, GLA_HEAD_K, GLA_HEAD_V), F32)
    x, gla_state_prompt = _gla_prompt(x, zero_state, ng(1, 0), ng(1, 1), gw, wgu, bg, ong, wo, tri, level,
                                      sign)
    y_prompt = _ffn(x.reshape(b * l, d), ng(1, 2), ng(1, 3), w13, w2, 1).reshape(b, l, d)

    xs = x_sample.reshape(n, d)
    xs, sg_v_sample = _sgu_sample(xs, ng(0, 0), ng(0, 1), win, lng, lnb, wdiag, bias0, wout)
    xs = _ffn(xs, ng(0, 2), ng(0, 3), w13, w2, 0)
    xs, gla_state_sample = _gla_sample(xs, state_gla[0].astype(F32), ng(1, 0), ng(1, 1), gw, gw.T, wgu.T,
                                       bg.reshape(-1, 1), ong, wo)
    y_sample = _ffn(xs, ng(1, 2), ng(1, 3), w13, w2, 1).reshape(n, 1, d)

    return (y_prompt, y_sample, sg_v_prompt[None], sg_v_sample.reshape(1, n, 1, SG_WIDTH),
            gla_state_prompt[None].astype(x_prompt.dtype), gla_state_sample[None].astype(state_gla.dtype))
```
